```python
import math
import jax
import jax.numpy as jnp
from jax import lax
import numpy as np

D_MODEL = 1024
BATCH = 8
SEQ = 2048
DEPTH = 1
DEC_BATCH = 128
DEC_SEQ = 4
PAST_LEN = 8192
PAGE_SIZE = 128

N_MEM = 256
H_A = 8
NOPE = 128
ROPE = 64
V_A = 128
Q_LORA = 512
KV_LORA = 256
H_F = 8
HD_F = 64
H_M = 4
HD_M = 128
D_FF = ((8 * D_MODEL + 3 * 256 - 1) // (3 * 256)) * 256
ROPE_THETA = 10000.0
QBLK = 128
LN_EPS = 1e-5
RMS_EPS = 1e-6
ALPHA = (2 * DEPTH) ** 0.25
BETA = (8 * DEPTH) ** -0.25
MLA_SCALE = (NOPE + ROPE) ** -0.5
FOX_SCALE = HD_F ** -0.5
MEM_SCALE = HD_M ** -0.5
IN_SPLITS = (Q_LORA, KV_LORA, ROPE, H_F * HD_F, H_F * HD_F, H_F * HD_F, H_F, H_M * HD_M, 3 * D_MODEL)
IN_DIM = sum(IN_SPLITS)

kernel_name = 'hybrid_mla_fox_memory_decoder_step'


def _layernorm(x, g, b):
    x32 = x.astype(jnp.float32)
    mu = jnp.mean(x32, axis=-1, keepdims=True)
    var = jnp.mean(jnp.square(x32 - mu), axis=-1, keepdims=True)
    out = (x32 - mu) * lax.rsqrt(var + LN_EPS) * g.astype(jnp.float32) + b.astype(jnp.float32)
    return out.astype(x.dtype)


def _rmsnorm(x, g):
    x32 = x.astype(jnp.float32)
    out = x32 * lax.rsqrt(jnp.mean(jnp.square(x32), axis=-1, keepdims=True) + RMS_EPS) * g.astype(jnp.float32)
    return out.astype(x.dtype)


def _rope(x, pos):
    half = x.shape[-1] // 2
    inv = jnp.exp(-math.log(ROPE_THETA) * jnp.arange(half, dtype=jnp.float32) / half)
    ang = pos.astype(jnp.float32)[:, None] * inv
    shp = (ang.shape[0],) + (1,) * (x.ndim - 3) + (half,)
    cos, sin = jnp.cos(ang).reshape(shp), jnp.sin(ang).reshape(shp)
    x32 = x.astype(jnp.float32)
    x1, x2 = x32[..., :half], x32[..., half:]
    return jnp.concatenate([x1 * cos - x2 * sin, x2 * cos + x1 * sin], axis=-1).astype(x.dtype)


def _sweep(block_fn, q_pos, *q_args):
    sq = q_pos.shape[0]
    qb = min(QBLK, sq)
    nb = sq // qb

    def split(a):
        return jnp.moveaxis(a.reshape((a.shape[0], nb, qb) + a.shape[2:]), 1, 0)

    out = lax.map(lambda args: block_fn(*args), (q_pos.reshape(nb, qb),) + tuple(split(a) for a in q_args))
    out = jnp.moveaxis(out, 0, 1)
    return out.reshape((out.shape[0], sq) + out.shape[3:])


def _masked_softmax(logits, q_pos, k_pos):
    mask = k_pos[None, :] <= q_pos[:, None]
    return jax.nn.softmax(jnp.where(mask, logits, -jnp.inf), axis=-1)


def _mla_attend(q_pos, q_lat, q_rope, segs):
    k_pos = jnp.concatenate([s[0] for s in segs])

    def block(qp, ql, qr):
        logits = jnp.concatenate(
            [(jnp.einsum('bqhr,bkr->bhqk', ql, lat) + jnp.einsum('bqhd,bkd->bhqk', qr, kr)).astype(jnp.float32)
             for _, lat, kr in segs], axis=-1) * MLA_SCALE
        p = _masked_softmax(logits, qp, k_pos)
        out, off = 0.0, 0
        for _, lat, _ in segs:
            n = lat.shape[1]
            out = out + jnp.einsum('bhqk,bkr->bqhr', p[..., off:off + n].astype(lat.dtype), lat)
            off += n
        return out

    return _sweep(block, q_pos, q_lat, q_rope)


def _fox_attend(q_pos, q, c_q, segs):
    k_pos = jnp.concatenate([s[0] for s in segs])

    def block(qp, qb, cqb):
        parts = [jnp.einsum('bqhd,bkhd->bhqk', qb, k).astype(jnp.float32) * FOX_SCALE
                 - jnp.swapaxes(ck, 1, 2)[:, :, None, :] for _, k, _, ck in segs]
        logits = jnp.concatenate(parts, axis=-1) + jnp.swapaxes(cqb, 1, 2)[..., None]
        p = _masked_softmax(logits, qp, k_pos)
        out, off = 0.0, 0
        for _, _, v, _ in segs:
            n = v.shape[1]
            out = out + jnp.einsum('bhqk,bkhd->bqhd', p[..., off:off + n].astype(v.dtype), v)
            off += n
        return out

    return _sweep(block, q_pos, q, c_q)


def _mem_attend(q, mem_k, mem_v):
    logits = jnp.einsum('bqhd,bmhd->bhqm', q, mem_k).astype(jnp.float32) * MEM_SCALE
    p = jax.nn.softmax(logits, axis=-1)
    return jnp.einsum('bhqm,bmhd->bqhd', p.astype(mem_v.dtype), mem_v)


def _swiglu(h, w_gate, w_up, w_down):
    a = jax.nn.silu(jnp.einsum('bsd,df->bsf', h, w_gate)) * jnp.einsum('bsd,df->bsf', h, w_up)
    return jnp.einsum('bsf,fd->bsd', a, w_down)


def _gather_pages(pool, page_table):
    g = pool[page_table]
    return g.reshape((page_table.shape[0], page_table.shape[1] * pool.shape[1]) + pool.shape[2:])


def _layer(x, pos, past, mem_k, mem_v, w):
    (W_in, b_forget, b_gate, q_norm_g, W_uq, kv_norm_g, W_uk, W_uv,
     W_br_a, W_br_f, W_br_m, W_out, ln1_g, ln1_b, w_gate, w_up, w_down, ln2_g, ln2_b) = w
    B, S = x.shape[0], x.shape[1]
    offs = np.cumsum(IN_SPLITS)[:-1].tolist()
    zq, zkv, zkr, zfq, zfk, zfv, zf, zmq, zg = jnp.split(jnp.einsum('bsd,de->bse', x, W_in), offs, axis=-1)
    q = jnp.einsum('bsr,rhe->bshe', _rmsnorm(zq, q_norm_g), W_uq)
    q_rope = _rope(q[..., NOPE:], pos)
    q_lat = jnp.einsum('bshn,rhn->bshr', q[..., :NOPE], W_uk)
    lat = _rmsnorm(zkv, kv_norm_g)
    krope = _rope(zkr, pos)
    fq = zfq.reshape(B, S, H_F, HD_F)
    fk = zfk.reshape(B, S, H_F, HD_F)
    fv = zfv.reshape(B, S, H_F, HD_F)
    logf = jax.nn.log_sigmoid(zf.astype(jnp.float32) + b_forget.astype(jnp.float32))
    c_new = jnp.cumsum(logf, axis=1)
    mq = zmq.reshape(B, S, H_M, HD_M)
    gates = jax.nn.sigmoid(zg.reshape(B, S, 3, D_MODEL) + b_gate)
    if past is None:
        mla_segs = ((pos, lat, krope),)
        fox_segs = ((pos, fk, fv, c_new),)
    else:
        p_pos, p_lat, p_kr, p_fk, p_fv, p_logf = past
        c_past = jnp.cumsum(p_logf.astype(jnp.float32), axis=1)
        c_new = c_new + c_past[:, -1:]
        mla_segs = ((p_pos, p_lat, p_kr), (pos, lat, krope))
        fox_segs = ((p_pos, p_fk, p_fv, c_past), (pos, fk, fv, c_new))
    o_a = jnp.einsum('bshr,rhv->bshv', _mla_attend(pos, q_lat, q_rope, mla_segs), W_uv).reshape(B, S, H_A * V_A)
    o_f = _fox_attend(pos, fq, c_new, fox_segs).reshape(B, S, H_F * HD_F)
    o_m = _mem_attend(mq, mem_k, mem_v).reshape(B, S, H_M * HD_M)
    merged = (gates[:, :, 0] * jnp.einsum('bse,ed->bsd', o_a, W_br_a)
              + gates[:, :, 1] * jnp.einsum('bse,ed->bsd', o_f, W_br_f)
              + gates[:, :, 2] * jnp.einsum('bse,ed->bsd', o_m, W_br_m))
    h = _layernorm(ALPHA * x + jnp.einsum('bsd,de->bse', merged, W_out), ln1_g, ln1_b)
    y = _layernorm(ALPHA * h + _swiglu(h, w_gate, w_up, w_down), ln2_g, ln2_b)
    return y, (lat, krope, fk, fv, logf)


def setup_inputs(seed: int = 0) -> dict:
    key = jax.random.key(seed)
    ks = iter(jax.random.split(key, 40))
    f32 = jnp.float32

    def nrm(shape, scale=1.0):
        return jax.random.normal(next(ks), shape, f32) * scale

    n_pages = PAST_LEN // PAGE_SIZE
    n_pool = (5 * DEC_BATCH * n_pages) // 4
    d = {}
    d['x_prompt'] = nrm((BATCH, SEQ, D_MODEL))
    d['x_sample'] = nrm((DEC_BATCH, DEC_SEQ, D_MODEL))
    d['cache_mla_latent'] = nrm((n_pool, PAGE_SIZE, KV_LORA))
    d['cache_mla_krope'] = nrm((n_pool, PAGE_SIZE, ROPE))
    d['cache_fox_k'] = nrm((n_pool, PAGE_SIZE, H_F, HD_F))
    d['cache_fox_v'] = nrm((n_pool, PAGE_SIZE, H_F, HD_F))
    d['cache_fox_logf'] = jax.nn.log_sigmoid(3.0 + nrm((n_pool, PAGE_SIZE, H_F)))
    d['cache_mem_k'] = nrm((DEC_BATCH, N_MEM, H_M, HD_M))
    d['cache_mem_v'] = nrm((DEC_BATCH, N_MEM, H_M, HD_M))
    perm = jax.random.permutation(next(ks), n_pool)
    d['page_table'] = perm[:DEC_BATCH * n_pages].reshape(DEC_BATCH, n_pages).astype(jnp.int32)
    d['mem_prompt'] = nrm((BATCH, N_MEM, D_MODEL))
    d['W_in'] = nrm((D_MODEL, IN_DIM), D_MODEL ** -0.5)
    d['b_forget'] = 3.0 + nrm((H_F,))
    d['b_gate'] = nrm((3, D_MODEL), 0.02)
    d['q_norm_g'] = 1.0 + nrm((Q_LORA,), 0.02)
    d['W_uq'] = nrm((Q_LORA, H_A, NOPE + ROPE), Q_LORA ** -0.5)
    d['kv_norm_g'] = 1.0 + nrm((KV_LORA,), 0.02)
    d['W_uk'] = nrm((KV_LORA, H_A, NOPE), KV_LORA ** -0.5)
    d['W_uv'] = nrm((KV_LORA, H_A, V_A), BETA * KV_LORA ** -0.5)
    d['W_mem_k'] = nrm((D_MODEL, H_M, HD_M), D_MODEL ** -0.5)
    d['W_mem_v'] = nrm((D_MODEL, H_M, HD_M), BETA * D_MODEL ** -0.5)
    d['W_br_a'] = nrm((H_A * V_A, D_MODEL), BETA * (H_A * V_A) ** -0.5)
    d['W_br_f'] = nrm((H_F * HD_F, D_MODEL), BETA * (H_F * HD_F) ** -0.5)
    d['W_br_m'] = nrm((H_M * HD_M, D_MODEL), BETA * (H_M * HD_M) ** -0.5)
    d['W_out'] = nrm((D_MODEL, D_MODEL), BETA * D_MODEL ** -0.5)
    d['ln1_g'] = 1.0 + nrm((D_MODEL,), 0.02)
    d['ln1_b'] = nrm((D_MODEL,), 0.02)
    d['w_gate'] = nrm((D_MODEL, D_FF), D_MODEL ** -0.5)
    d['w_up'] = nrm((D_MODEL, D_FF), BETA * D_MODEL ** -0.5)
    d['w_down'] = nrm((D_FF, D_MODEL), BETA * D_FF ** -0.5)
    d['ln2_g'] = 1.0 + nrm((D_MODEL,), 0.02)
    d['ln2_b'] = nrm((D_MODEL,), 0.02)
    return d


def reference(x_prompt, x_sample, cache_mla_latent, cache_mla_krope, cache_fox_k, cache_fox_v,
              cache_fox_logf, cache_mem_k, cache_mem_v, page_table, mem_prompt,
              W_in, b_forget, b_gate, q_norm_g, W_uq, kv_norm_g, W_uk, W_uv, W_mem_k, W_mem_v,
              W_br_a, W_br_f, W_br_m, W_out, ln1_g, ln1_b, w_gate, w_up, w_down, ln2_g, ln2_b):
    w = (W_in, b_forget, b_gate, q_norm_g, W_uq, kv_norm_g, W_uk, W_uv,
         W_br_a, W_br_f, W_br_m, W_out, ln1_g, ln1_b, w_gate, w_up, w_down, ln2_g, ln2_b)
    pos_p = jnp.arange(x_prompt.shape[1], dtype=jnp.int32)
    mem_k_p = jnp.einsum('bmd,dhe->bmhe', mem_prompt, W_mem_k)
    mem_v_p = jnp.einsum('bmd,dhe->bmhe', mem_prompt, W_mem_v)
    y_prompt, (lat_p, krope_p, fox_k_p, fox_v_p, fox_logf_p) = _layer(x_prompt, pos_p, None, mem_k_p, mem_v_p, w)
    past_len = page_table.shape[1] * cache_mla_latent.shape[1]
    pos_s = past_len + jnp.arange(x_sample.shape[1], dtype=jnp.int32)
    past = (jnp.arange(past_len, dtype=jnp.int32),
            _gather_pages(cache_mla_latent, page_table),
            _gather_pages(cache_mla_krope, page_table),
            _gather_pages(cache_fox_k, page_table),
            _gather_pages(cache_fox_v, page_table),
            _gather_pages(cache_fox_logf, page_table))
    y_sample, (lat_s, krope_s, fox_k_s, fox_v_s, fox_logf_s) = _layer(x_sample, pos_s, past, cache_mem_k, cache_mem_v, w)
    return (y_prompt, y_sample, lat_p, krope_p, fox_k_p, fox_v_p, fox_logf_p, mem_k_p, mem_v_p,
            lat_s, krope_s, fox_k_s, fox_v_s, fox_logf_s)
```

```python
import functools
import math

import jax
import jax.numpy as jnp
from jax import lax
from jax.experimental import pallas as pl
from jax.experimental.pallas import tpu as pltpu

F32 = jnp.float32
BF16 = jnp.bfloat16

H_A, NOPE, ROPE, V_A = 8, 128, 64, 128
Q_LORA, KV_LORA = 512, 256
H_F, HD_F = 8, 64
H_M, HD_M = 4, 128
D_MODEL = 1024
ROPE_THETA = 10000.0
LN_EPS = 1e-5
RMS_EPS = 1e-6
DEPTH = 1
ALPHA = (2 * DEPTH) ** 0.25
MLA_SCALE = (NOPE + ROPE) ** -0.5
FOX_SCALE = HD_F ** -0.5
MEM_SCALE = HD_M ** -0.5

LANES = 128
QCAT = KV_LORA + LANES
NEG_INF = float("-inf")

ROW_TILE = 256
ATT_TILE = 512
CUMSUM_CHUNK = 256
POOL_PAGES_PER_STEP = 256
DECODE_PAGES_PER_STEP = 8


def _dot(a, b):
    return jnp.dot(a, b, preferred_element_type=F32)


def _dot_nt(a, b):
    return lax.dot_general(a, b, (((1,), (1,)), ((), ())), preferred_element_type=F32)


def _const_spec(shape):
    zeros = (0,) * len(shape)
    return pl.BlockSpec(shape, lambda *_: zeros, pipeline_mode=pl.Buffered(1))


def _split3(x):
    hi = x.astype(BF16).astype(F32)
    r = x - hi
    mid = r.astype(BF16).astype(F32)
    lo = (r - mid).astype(BF16).astype(F32)
    return hi, mid, lo


def _exact_dot_ones(x, tri):
    hi, mid, lo = _split3(x)
    return _dot(hi, tri) + _dot(mid, tri) + _dot(lo, tri)


def _layernorm(v, g, b):
    mu = jnp.mean(v, axis=-1, keepdims=True)
    d = v - mu
    var = jnp.mean(d * d, axis=-1, keepdims=True)
    return d * lax.rsqrt(var + LN_EPS) * g + b


def _rmsnorm(v, g):
    return v * lax.rsqrt(jnp.mean(v * v, axis=-1, keepdims=True) + RMS_EPS) * g


def _sigmoid(v):
    return 1.0 / (1.0 + jnp.exp(-v))


def _rope_tile(x, cos, sin_signed):
    lane = lax.broadcasted_iota(jnp.int32, x.shape, 1)
    rot = jnp.where(lane < ROPE // 2, pltpu.roll(x, LANES - ROPE // 2, 1), pltpu.roll(x, ROPE // 2, 1))
    return x * cos + rot * sin_signed


def _front_kernel(x_ref, cos_ref, sin_ref, w_ref, wuq_ref, wuk_ref, qg_ref, kvg_ref, bf_ref,
                  lat_ref, krope_ref, fk_ref, fv_ref, logf_ref,
                  qcat_ref, kcat_ref, fq_ref, fk16_ref, fv16_ref, mq_ref):
    xb = x_ref[...].astype(BF16)
    z = _dot(xb, w_ref[...])
    cos = cos_ref[...]
    sin = sin_ref[...]
    qn = _rmsnorm(z[:, 0:512], qg_ref[...]).astype(BF16)
    q = _dot(qn, wuq_ref[...])
    for h in range(H_A):
        q_nope = q[:, h * NOPE:(h + 1) * NOPE].astype(BF16)
        qcat_ref[:, h * QCAT:h * QCAT + KV_LORA] = _dot(q_nope, wuk_ref[h]).astype(BF16)
        xr = q[:, H_A * NOPE + h * LANES:H_A * NOPE + (h + 1) * LANES]
        qcat_ref[:, h * QCAT + KV_LORA:(h + 1) * QCAT] = _rope_tile(xr, cos, sin).astype(BF16)
    lat = _rmsnorm(z[:, 512:768], kvg_ref[...])
    lat_ref[...] = lat
    kcat_ref[:, 0:KV_LORA] = lat.astype(BF16)
    kr = _rope_tile(z[:, 2816:2944], cos, sin)
    krope_ref[...] = kr[:, 0:ROPE]
    kcat_ref[:, KV_LORA:QCAT] = kr.astype(BF16)
    fq_ref[...] = (z[:, 768:1280] * FOX_SCALE).astype(BF16)
    zfk = z[:, 1280:1792]
    zfv = z[:, 1792:2304]
    fk_ref[...] = zfk
    fv_ref[...] = zfv
    fk16_ref[...] = zfk.astype(BF16)
    fv16_ref[...] = zfv.astype(BF16)
    mq_ref[...] = z[:, 2304:2816].astype(BF16)
    zf = z[:, 2944:3072] + bf_ref[...]
    logf = jnp.minimum(zf, 0.0) - jnp.log1p(jnp.exp(-jnp.abs(zf)))
    logf_ref[...] = logf[:, 0:H_F]


def _front(x2d, cos_t, sin_t, wp):
    t = x2d.shape[0]
    tm = min(ROW_TILE, t)
    nblk = t // tm
    ntab = cos_t.shape[0] // tm
    row = lambda n: pl.BlockSpec((tm, n), lambda i: (i, 0))
    tab = pl.BlockSpec((tm, LANES), lambda i: (i % ntab, 0))
    out_shape = (
        jax.ShapeDtypeStruct((t, KV_LORA), F32), jax.ShapeDtypeStruct((t, ROPE), F32),
        jax.ShapeDtypeStruct((t, H_F * HD_F), F32), jax.ShapeDtypeStruct((t, H_F * HD_F), F32),
        jax.ShapeDtypeStruct((t, H_F), F32),
        jax.ShapeDtypeStruct((t, H_A * QCAT), BF16), jax.ShapeDtypeStruct((t, QCAT), BF16),
        jax.ShapeDtypeStruct((t, H_F * HD_F), BF16), jax.ShapeDtypeStruct((t, H_F * HD_F), BF16),
        jax.ShapeDtypeStruct((t, H_F * HD_F), BF16), jax.ShapeDtypeStruct((t, H_M * HD_M), BF16),
    )
    out_specs = (row(KV_LORA), row(ROPE), row(512), row(512), row(H_F),
                 row(H_A * QCAT), row(QCAT), row(512), row(512), row(512), row(512))
    return pl.pallas_call(
        _front_kernel,
        grid=(nblk,),
        in_specs=[row(D_MODEL), tab, tab,
                  _const_spec(wp["w_front"].shape), _const_spec(wp["w_uq"].shape),
                  _const_spec(wp["w_uk"].shape), _const_spec((1, Q_LORA)),
                  _const_spec((1, KV_LORA)), _const_spec((1, LANES))],
        out_specs=out_specs,
        out_shape=out_shape,
        compiler_params=pltpu.CompilerParams(dimension_semantics=("parallel",)),
        name="front",
    )(x2d, cos_t, sin_t, wp["w_front"], wp["w_uq"], wp["w_uk"], wp["q_g"], wp["kv_g"], wp["b_f"])


def _mm_kernel(x_ref, w_ref, o_ref):
    o_ref[...] = _dot(x_ref[...].astype(BF16), w_ref[...])


def _matmul(x2d, w):
    t, k = x2d.shape
    n = w.shape[1]
    tm = min(ROW_TILE, t)
    return pl.pallas_call(
        _mm_kernel,
        grid=(t // tm,),
        in_specs=[pl.BlockSpec((tm, k), lambda i: (i, 0)), _const_spec((k, n))],
        out_specs=pl.BlockSpec((tm, n), lambda i: (i, 0)),
        out_shape=jax.ShapeDtypeStruct((t, n), F32),
        compiler_params=pltpu.CompilerParams(dimension_semantics=("parallel",)),
        name="mem_kv_proj",
    )(x2d, w)


def _cumsum_kernel(x_ref, o_ref):
    n = x_ref.shape[1]
    ck = min(CUMSUM_CHUNK, n)
    r = lax.broadcasted_iota(jnp.int32, (ck, ck), 0)
    c = lax.broadcasted_iota(jnp.int32, (ck, ck), 1)
    tri = (r <= c).astype(F32)
    carry = jnp.zeros((x_ref.shape[0], 1), F32)
    for j in range(n // ck):
        cs = _exact_dot_ones(x_ref[:, j * ck:(j + 1) * ck], tri) + carry
        o_ref[:, j * ck:(j + 1) * ck] = cs
        carry = cs[:, ck - 1:ck]


def _cumsum_lanes(x2d):
    return pl.pallas_call(
        _cumsum_kernel,
        out_shape=jax.ShapeDtypeStruct(x2d.shape, F32),
        name="logf_cumsum",
    )(x2d)


def _online_update(s, v, m_ref, l_ref, acc_ref, idx):
    m_prev = m_ref[idx]
    m_new = jnp.maximum(m_prev, jnp.max(s, axis=-1, keepdims=True))
    alpha = jnp.exp(m_prev - m_new)
    p = jnp.exp(s - m_new)
    l_ref[idx] = alpha * l_ref[idx] + jnp.sum(p, axis=-1, keepdims=True)
    acc_ref[idx] = alpha * acc_ref[idx] + _dot(p.astype(BF16), v)
    m_ref[idx] = m_new


def _causal_mask(s, qi, ki):
    tq, tk = s.shape
    rows = qi * tq + lax.broadcasted_iota(jnp.int32, s.shape, 0)
    cols = ki * tk + lax.broadcasted_iota(jnp.int32, s.shape, 1)
    return jnp.where(cols <= rows, s, NEG_INF)


def _mla_flash_kernel(q_ref, k_ref, o_ref, m_ref, l_ref, acc_ref):
    qi = pl.program_id(2)
    ki = pl.program_id(3)

    @pl.when(ki == 0)
    def _():
        m_ref[...] = jnp.full(m_ref.shape, NEG_INF, F32)
        l_ref[...] = jnp.zeros(l_ref.shape, F32)
        acc_ref[...] = jnp.zeros(acc_ref.shape, F32)

    def step(masked):
        k = k_ref[0]
        s = _dot_nt(q_ref[0], k) * MLA_SCALE
        if masked:
            s = _causal_mask(s, qi, ki)
        _online_update(s, k[:, 0:KV_LORA], m_ref, l_ref, acc_ref, 0)

    @pl.when(ki < qi)
    def _():
        step(False)

    @pl.when(ki == qi)
    def _():
        step(True)
        o_ref[0] = (acc_ref[0] / l_ref[0]).astype(o_ref.dtype)


def _mla_flash(qcat, kcat):
    b, s, _ = qcat.shape
    t = min(ATT_TILE, s)
    n = s // t
    return pl.pallas_call(
        _mla_flash_kernel,
        grid=(b, H_A, n, n),
        in_specs=[pl.BlockSpec((1, t, QCAT), lambda b, h, qi, ki: (b, qi, h)),
                  pl.BlockSpec((1, t, QCAT), lambda b, h, qi, ki: (b, jnp.minimum(ki, qi), 0))],
        out_specs=pl.BlockSpec((1, t, KV_LORA), lambda b, h, qi, ki: (b, qi, h)),
        out_shape=jax.ShapeDtypeStruct((b, s, H_A * KV_LORA), BF16),
        scratch_shapes=[pltpu.VMEM((1, t, 1), F32), pltpu.VMEM((1, t, 1), F32),
                        pltpu.VMEM((1, t, KV_LORA), F32)],
        compiler_params=pltpu.CompilerParams(
            dimension_semantics=("parallel", "parallel", "parallel", "arbitrary")),
        name="mla_prompt_attn",
    )(qcat, kcat)


def _fox_flash_kernel(q_ref, k_ref, v_ref, cq_ref, ck_ref, o_ref, m_ref, l_ref, acc_ref):
    qi = pl.program_id(2)
    ki = pl.program_id(3)

    @pl.when(ki == 0)
    def _():
        m_ref[...] = jnp.full(m_ref.shape, NEG_INF, F32)
        l_ref[...] = jnp.zeros(l_ref.shape, F32)
        acc_ref[...] = jnp.zeros(acc_ref.shape, F32)

    def step(masked):
        q2 = q_ref[0]
        k2 = k_ref[0]
        v2 = v_ref[0]
        lane = lax.broadcasted_iota(jnp.int32, q2.shape, 1)
        cq = cq_ref[0, 0]
        ck = ck_ref[0, 0]
        for hh in range(2):
            qm = jnp.where((lane >= HD_F) == (hh == 1), q2, jnp.zeros_like(q2))
            s = _dot_nt(qm, k2) + cq[:, hh:hh + 1] - ck[hh:hh + 1, :]
            if masked:
                s = _causal_mask(s, qi, ki)
            _online_update(s, v2, m_ref, l_ref, acc_ref, hh)

    @pl.when(ki < qi)
    def _():
        step(False)

    @pl.when(ki == qi)
    def _():
        step(True)
        lane = lax.broadcasted_iota(jnp.int32, acc_ref.shape[1:], 1)
        o0 = acc_ref[0] / l_ref[0]
        o1 = acc_ref[1] / l_ref[1]
        o_ref[0] = jnp.where(lane < HD_F, o0, o1).astype(o_ref.dtype)


def _fox_flash(fq, fk16, fv16, c_col, c_row):
    b, s, _ = fq.shape
    t = min(ATT_TILE, s)
    n = s // t
    npair = H_F // 2
    qspec = pl.BlockSpec((1, t, LANES), lambda b, h, qi, ki: (b, qi, h))
    kspec = pl.BlockSpec((1, t, LANES), lambda b, h, qi, ki: (b, jnp.minimum(ki, qi), h))
    return pl.pallas_call(
        _fox_flash_kernel,
        grid=(b, npair, n, n),
        in_specs=[qspec, kspec, kspec,
                  pl.BlockSpec((1, 1, t, 2), lambda b, h, qi, ki: (b, h, qi, 0)),
                  pl.BlockSpec((1, 1, 2, t), lambda b, h, qi, ki: (b, h, 0, jnp.minimum(ki, qi)))],
        out_specs=qspec,
        out_shape=jax.ShapeDtypeStruct((b, s, H_F * HD_F), BF16),
        scratch_shapes=[pltpu.VMEM((2, t, 1), F32), pltpu.VMEM((2, t, 1), F32),
                        pltpu.VMEM((2, t, LANES), F32)],
        compiler_params=pltpu.CompilerParams(
            dimension_semantics=("parallel", "parallel", "parallel", "arbitrary")),
        name="fox_prompt_attn",
    )(fq, fk16, fv16, c_col, c_row)


def _mem_attn_kernel(q_ref, k_ref, v_ref, o_ref):
    s = _dot_nt(q_ref[0], k_ref[0].astype(BF16)) * MEM_SCALE
    m = jnp.max(s, axis=-1, keepdims=True)
    p = jnp.exp(s - m)
    l = jnp.sum(p, axis=-1, keepdims=True)
    o_ref[0] = (_dot(p.astype(BF16), v_ref[0].astype(BF16)) / l).astype(o_ref.dtype)


def _mem_attn(mq, mem_k, mem_v):
    b, s, _ = mq.shape
    nm = mem_k.shape[1]
    t = min(ATT_TILE, s)
    return pl.pallas_call(
        _mem_attn_kernel,
        grid=(b, H_M, s // t),
        in_specs=[pl.BlockSpec((1, t, HD_M), lambda b, h, qi: (b, qi, h)),
                  pl.BlockSpec((1, nm, HD_M), lambda b, h, qi: (b, 0, h)),
                  pl.BlockSpec((1, nm, HD_M), lambda b, h, qi: (b, 0, h))],
        out_specs=pl.BlockSpec((1, t, HD_M), lambda b, h, qi: (b, qi, h)),
        out_shape=jax.ShapeDtypeStruct((b, s, H_M * HD_M), BF16),
        compiler_params=pltpu.CompilerParams(
            dimension_semantics=("parallel", "parallel", "parallel")),
        name="mem_prompt_attn",
    )(mq, mem_k, mem_v)


def _merge_kernel(x_ref, olat_ref, of_ref, om_ref, wg_ref, bg_ref, wuv_ref, wa_ref, wf_ref, wm_ref,
                  wo_ref, g_ref, b_ref, h_ref, oa_ref):
    x = x_ref[...]
    xb = x.astype(BF16)
    for h in range(H_A):
        oa_ref[:, h * V_A:(h + 1) * V_A] = _dot(
            olat_ref[:, h * KV_LORA:(h + 1) * KV_LORA], wuv_ref[h]).astype(BF16)
    branches = (_dot(oa_ref[...], wa_ref[...]), _dot(of_ref[...], wf_ref[...]),
                _dot(om_ref[...], wm_ref[...]))
    merged = None
    for i, br in enumerate(branches):
        gate = _sigmoid(_dot(xb, wg_ref[:, i * D_MODEL:(i + 1) * D_MODEL])
                        + bg_ref[:, i * D_MODEL:(i + 1) * D_MODEL])
        merged = gate * br if merged is None else merged + gate * br
    pre = ALPHA * x + _dot(merged.astype(BF16), wo_ref[...])
    h_ref[...] = _layernorm(pre, g_ref[...], b_ref[...])


def _merge(x2d, o_lat, o_f, o_m, wp):
    t = x2d.shape[0]
    tm = min(ROW_TILE, t)
    row = lambda n: pl.BlockSpec((tm, n), lambda i: (i, 0))
    names = ("w_g", "b_g", "w_uv", "w_br_a", "w_br_f", "w_br_m", "w_out", "ln1_g", "ln1_b")
    return pl.pallas_call(
        _merge_kernel,
        grid=(t // tm,),
        in_specs=[row(D_MODEL), row(H_A * KV_LORA), row(H_F * HD_F), row(H_M * HD_M)]
        + [_const_spec(wp[n].shape) for n in names],
        out_specs=row(D_MODEL),
        out_shape=jax.ShapeDtypeStruct((t, D_MODEL), F32),
        scratch_shapes=[pltpu.VMEM((tm, H_A * V_A), BF16)],
        compiler_params=pltpu.CompilerParams(dimension_semantics=("parallel",)),
        name="merge_ln1",
    )(x2d, o_lat, o_f, o_m, *[wp[n] for n in names])


def _mlp_kernel(h_ref, wg_ref, wu_ref, wd_ref, g_ref, b_ref, y_ref, a_ref, *, n_chunks):
    h = h_ref[...]
    hb = h.astype(BF16)
    fc = wg_ref.shape[1] // n_chunks
    for c in range(n_chunks):
        gt = _dot(hb, wg_ref[:, c * fc:(c + 1) * fc])
        up = _dot(hb, wu_ref[:, c * fc:(c + 1) * fc])
        a_ref[:, c * fc:(c + 1) * fc] = (gt * _sigmoid(gt) * up).astype(BF16)
    y_ref[...] = _layernorm(ALPHA * h + _dot(a_ref[...], wd_ref[...]), g_ref[...], b_ref[...])


def _mlp(h2d, wp):
    t = h2d.shape[0]
    tm = min(ROW_TILE, t)
    d_ff = wp["w_gate"].shape[1]
    n_chunks = 2
    row = lambda n: pl.BlockSpec((tm, n), lambda i: (i, 0))
    names = ("w_gate", "w_up", "w_down", "ln2_g", "ln2_b")
    return pl.pallas_call(
        functools.partial(_mlp_kernel, n_chunks=n_chunks),
        grid=(t // tm,),
        in_specs=[row(D_MODEL)] + [_const_spec(wp[n].shape) for n in names],
        out_specs=row(D_MODEL),
        out_shape=jax.ShapeDtypeStruct((t, D_MODEL), F32),
        scratch_shapes=[pltpu.VMEM((tm, d_ff), BF16)],
        compiler_params=pltpu.CompilerParams(dimension_semantics=("parallel",)),
        name="mlp_ln2",
    )(h2d, *[wp[n] for n in names])


def _pool_suffix_kernel(x_ref, o_ref):
    pb = x_ref.shape[0]
    l = x_ref[...].reshape(pb * H_F, LANES)
    r = lax.broadcasted_iota(jnp.int32, (LANES, LANES), 0)
    c = lax.broadcasted_iota(jnp.int32, (LANES, LANES), 1)
    excl = _exact_dot_ones(l, (r > c).astype(F32))
    tot = jnp.broadcast_to(excl[:, 0:1] + l[:, 0:1], excl.shape)
    o_ref[:, 0:H_F, :] = excl.reshape(pb, H_F, LANES)
    o_ref[:, H_F:2 * H_F, :] = tot.reshape(pb, H_F, LANES)


def _pool_suffix(logf_t):
    n_pool = logf_t.shape[0]
    pb = min(POOL_PAGES_PER_STEP, n_pool)
    return pl.pallas_call(
        _pool_suffix_kernel,
        grid=(n_pool // pb,),
        in_specs=[pl.BlockSpec((pb, H_F, LANES), lambda i: (i, 0, 0))],
        out_specs=pl.BlockSpec((pb, 2 * H_F, LANES), lambda i: (i, 0, 0)),
        out_shape=jax.ShapeDtypeStruct((n_pool, 2 * H_F, LANES), F32),
        compiler_params=pltpu.CompilerParams(dimension_semantics=("parallel",)),
        name="pool_logf_suffix",
    )(logf_t)


def _rows_per_query(x, nq):
    return jnp.concatenate([jnp.broadcast_to(x[q:q + 1], (8, x.shape[1])) for q in range(nq)], axis=0)


def _decode_kernel(pt_ref, qcat_ref, fq_ref, latn_ref, krn_ref, fkn_ref, fvn_ref, lfn_ref, *rest,
                   ch, nq, nch):
    del pt_ref
    lat_v, kr_v, fk_v, fv_v, et_v = (rest[i * ch:(i + 1) * ch] for i in range(5))
    olat_ref, of_ref = rest[5 * ch:5 * ch + 2]
    (qbd_ref, cq_ref, carry_ref, m_ref, l_ref, acca_ref, accf_ref,
     latc_ref, krc_ref, fkc_ref, fvc_ref) = rest[5 * ch + 2:]
    j = pl.program_id(1)
    nrow = nq * 8
    page = lat_v[0].shape[1]

    row512 = lax.broadcasted_iota(jnp.int32, (nrow, H_F * HD_F), 0)
    lane512 = lax.broadcasted_iota(jnp.int32, (nrow, H_F * HD_F), 1)
    head_mask = (lane512 // HD_F) == (row512 % 8)

    @pl.when(j == 0)
    def _init():
        carry_ref[...] = jnp.zeros(carry_ref.shape, F32)
        qrow = lax.broadcasted_iota(jnp.int32, (nrow, 1), 0) // 8
        qbd = jnp.where(head_mask, _rows_per_query(fq_ref[0].astype(F32), nq), 0.0)
        qbd_ref[...] = qbd.astype(BF16)
        lfn = lfn_ref[0]
        r8 = lax.broadcasted_iota(jnp.int32, (nrow, H_F), 0)
        l8 = lax.broadcasted_iota(jnp.int32, (nrow, H_F), 1)
        pick = l8 == (r8 % 8)
        cum = []
        for q in range(nq):
            cum.append(lfn[q:q + 1] if q == 0 else cum[-1] + lfn[q:q + 1])
        cq = jnp.sum(jnp.where(pick, jnp.concatenate(
            [jnp.broadcast_to(c, (8, H_F)) for c in cum], axis=0), 0.0), axis=1, keepdims=True)
        cq_ref[...] = cq
        fkn = fkn_ref[0]
        fvn = fvn_ref[0]
        sf = []
        for s in range(nq):
            ck = jnp.sum(jnp.where(pick, jnp.broadcast_to(cum[s], (nrow, H_F)), 0.0),
                         axis=1, keepdims=True)
            v = jnp.sum(qbd * fkn[s:s + 1], axis=1, keepdims=True) + cq - ck
            sf.append(jnp.where(qrow >= s, v, NEG_INF))
        qc = qcat_ref[0].astype(F32)
        latn = latn_ref[0]
        krn = krn_ref[0]
        sa = []
        for s in range(nq):
            v = (jnp.sum(qc[:, 0:KV_LORA] * latn[s:s + 1], axis=1, keepdims=True)
                 + jnp.sum(qc[:, KV_LORA:KV_LORA + ROPE] * krn[s:s + 1], axis=1, keepdims=True))
            sa.append(jnp.where(qrow >= s, v * MLA_SCALE, NEG_INF))
        for idx, (sc, vals, acc_ref) in enumerate(((sa, latn, acca_ref), (sf, fvn, accf_ref))):
            m = sc[0]
            for s in range(1, nq):
                m = jnp.maximum(m, sc[s])
            l = jnp.zeros_like(m)
            acc = jnp.zeros(acc_ref.shape, F32)
            for s in range(nq):
                p = jnp.exp(sc[s] - m)
                l = l + p
                acc = acc + p * vals[s:s + 1]
            m_ref[idx] = m
            l_ref[idx] = l
            acc_ref[...] = acc

    for i in range(ch):
        sl = slice(i * page, (i + 1) * page)
        latc_ref[sl, :] = lat_v[i][0].astype(BF16)
        krc_ref[sl, :] = kr_v[i][0].astype(BF16)
        fkc_ref[sl, :] = fk_v[i][0].astype(BF16)
        fvc_ref[sl, :] = fv_v[i][0].astype(BF16)

    carry = carry_ref[...]
    pieces = [None] * ch
    for i in reversed(range(ch)):
        et = et_v[i][0]
        pieces[i] = et[0:H_F] + carry
        carry = carry + et[H_F:2 * H_F]
    carry_ref[...] = carry
    bias = jnp.concatenate(pieces, axis=1)
    bias = jnp.concatenate([bias] * nq, axis=0)

    def update(s, v, acc_ref, idx):
        m_prev = m_ref[idx]
        m_new = jnp.maximum(m_prev, jnp.max(s, axis=-1, keepdims=True))
        alpha = jnp.exp(m_prev - m_new)
        p = jnp.exp(s - m_new)
        l_ref[idx] = alpha * l_ref[idx] + jnp.sum(p, axis=-1, keepdims=True)
        acc_ref[...] = alpha * acc_ref[...] + _dot(p.astype(BF16), v)
        m_ref[idx] = m_new

    sf = _dot_nt(qbd_ref[...], fkc_ref[...]) + bias + cq_ref[...]
    update(sf, fvc_ref[...], accf_ref, 1)
    qc = qcat_ref[0]
    latc = latc_ref[...]
    sa = (_dot_nt(qc[:, 0:KV_LORA], latc)
          + _dot_nt(qc[:, KV_LORA:KV_LORA + ROPE], krc_ref[...])) * MLA_SCALE
    update(sa, latc, acca_ref, 0)

    @pl.when(j == nch - 1)
    def _fin():
        olat_ref[0] = (acca_ref[...] / l_ref[0]).astype(olat_ref.dtype)
        of = jnp.where(head_mask, accf_ref[...] / l_ref[1], 0.0)
        of_ref[0] = jnp.sum(of.reshape(nq, 8, H_F * HD_F), axis=1).astype(of_ref.dtype)


def _decode_attn(page_table, qcat_s, fq_s, lat_s, krope_s, fk_s, fv_s, logf_s,
                 pool_lat, pool_kr, pool_fk, pool_fv, pool_et):
    nb, npages = page_table.shape
    nq = fq_s.shape[1]
    page = pool_lat.shape[1]
    ch = min(DECODE_PAGES_PER_STEP, npages)
    nch = npages // ch
    nrow = nq * 8
    pt_flat = page_table.reshape(-1)

    def per_b(shape):
        return pl.BlockSpec((1,) + shape, lambda b, j, pt: (b,) + (0,) * len(shape))

    def page_specs(arr):
        blk = (1,) + arr.shape[1:]
        zeros = (0,) * (arr.ndim - 1)
        return [pl.BlockSpec(blk, lambda b, j, pt, i=i: (pt[b * npages + (nch - 1 - j) * ch + i],) + zeros)
                for i in range(ch)]

    pools = (pool_lat, pool_kr, pool_fk, pool_fv, pool_et)
    in_specs = [per_b((nrow, QCAT)), per_b((nq, H_F * HD_F)), per_b((nq, KV_LORA)), per_b((nq, ROPE)),
                per_b((nq, H_F * HD_F)), per_b((nq, H_F * HD_F)), per_b((nq, H_F))]
    args = [qcat_s, fq_s, lat_s, krope_s, fk_s, fv_s, logf_s]
    for arr in pools:
        in_specs += page_specs(arr)
        args += [arr] * ch
    grid_spec = pltpu.PrefetchScalarGridSpec(
        num_scalar_prefetch=1,
        grid=(nb, nch),
        in_specs=in_specs,
        out_specs=(per_b((nrow, KV_LORA)), per_b((nq, H_F * HD_F))),
        scratch_shapes=[
            pltpu.VMEM((nrow, H_F * HD_F), BF16), pltpu.VMEM((nrow, 1), F32),
            pltpu.VMEM((H_F, LANES), F32),
            pltpu.VMEM((2, nrow, 1), F32), pltpu.VMEM((2, nrow, 1), F32),
            pltpu.VMEM((nrow, KV_LORA), F32), pltpu.VMEM((nrow, H_F * HD_F), F32),
            pltpu.VMEM((ch * page, KV_LORA), BF16), pltpu.VMEM((ch * page, ROPE), BF16),
            pltpu.VMEM((ch * page, H_F * HD_F), BF16), pltpu.VMEM((ch * page, H_F * HD_F), BF16),
        ],
    )
    return pl.pallas_call(
        functools.partial(_decode_kernel, ch=ch, nq=nq, nch=nch),
        grid_spec=grid_spec,
        out_shape=(jax.ShapeDtypeStruct((nb, nrow, KV_LORA), BF16),
                   jax.ShapeDtypeStruct((nb, nq, H_F * HD_F), BF16)),
        compiler_params=pltpu.CompilerParams(dimension_semantics=("parallel", "arbitrary")),
        name="decode_attn",
    )(pt_flat, *args)


def _mem_decode_kernel(q_ref, k_ref, v_ref, o_ref, *, nq):
    nrow = nq * 8
    row = lax.broadcasted_iota(jnp.int32, (nrow, H_M * HD_M), 0)
    lane = lax.broadcasted_iota(jnp.int32, (nrow, H_M * HD_M), 1)
    head_mask = (lane // HD_M) == (row % 8)
    qbd = jnp.where(head_mask, _rows_per_query(q_ref[0].astype(F32), nq), 0.0).astype(BF16)
    s = _dot_nt(qbd, k_ref[0].astype(BF16)) * MEM_SCALE
    m = jnp.max(s, axis=-1, keepdims=True)
    p = jnp.exp(s - m)
    l = jnp.sum(p, axis=-1, keepdims=True)
    o = jnp.where(head_mask, _dot(p.astype(BF16), v_ref[0].astype(BF16)) / l, 0.0)
    o_ref[0] = jnp.sum(o.reshape(nq, 8, H_M * HD_M), axis=1).astype(o_ref.dtype)


def _mem_decode(mq_s, mem_k, mem_v):
    nb, nq, d = mq_s.shape
    nm = mem_k.shape[1]
    return pl.pallas_call(
        functools.partial(_mem_decode_kernel, nq=nq),
        grid=(nb,),
        in_specs=[pl.BlockSpec((1, nq, d), lambda b: (b, 0, 0)),
                  pl.BlockSpec((1, nm, d), lambda b: (b, 0, 0)),
                  pl.BlockSpec((1, nm, d), lambda b: (b, 0, 0))],
        out_specs=pl.BlockSpec((1, nq, d), lambda b: (b, 0, 0)),
        out_shape=jax.ShapeDtypeStruct((nb, nq, d), BF16),
        compiler_params=pltpu.CompilerParams(dimension_semantics=("parallel",)),
        name="mem_decode_attn",
    )(mq_s, mem_k, mem_v)


def _rope_tables(pos):
    half = ROPE // 2
    inv = jnp.exp(-math.log(ROPE_THETA) * jnp.arange(half, dtype=F32) / half)
    ang = pos.astype(F32)[:, None] * inv
    cos, sin = jnp.cos(ang), jnp.sin(ang)
    pad = jnp.zeros((pos.shape[0], LANES - ROPE), F32)
    return (jnp.concatenate([cos, cos, pad], axis=1), jnp.concatenate([-sin, sin, pad], axis=1))


def _prepare_weights(W_in, b_forget, b_gate, q_norm_g, W_uq, kv_norm_g, W_uk, W_uv,
                     W_br_a, W_br_f, W_br_m, W_out, ln1_g, ln1_b, w_gate, w_up, w_down, ln2_g, ln2_b):
    splits = (Q_LORA, KV_LORA, ROPE, H_F * HD_F, H_F * HD_F, H_F * HD_F, H_F, H_M * HD_M, 3 * D_MODEL)
    offs = [0]
    for n in splits:
        offs.append(offs[-1] + n)
    wq, wkv, wkr, wfq, wfk, wfv, wf, wmq, wg = (W_in[:, offs[i]:offs[i + 1]] for i in range(9))
    pad_to = lambda w: jnp.pad(w, ((0, 0), (0, LANES - w.shape[1])))
    wp = {}
    wp["w_front"] = jnp.concatenate([wq, wkv, wfq, wfk, wfv, wmq, pad_to(wkr), pad_to(wf)], axis=1).astype(BF16)
    w_nope = W_uq[:, :, :NOPE].reshape(Q_LORA, H_A * NOPE)
    w_rope = jnp.pad(W_uq[:, :, NOPE:], ((0, 0), (0, 0), (0, LANES - ROPE))).reshape(Q_LORA, H_A * LANES)
    wp["w_uq"] = jnp.concatenate([w_nope, w_rope], axis=1).astype(BF16)
    wp["w_uk"] = jnp.transpose(W_uk, (1, 2, 0)).astype(BF16)
    wp["w_uv"] = jnp.transpose(W_uv, (1, 0, 2)).astype(BF16)
    wp["q_g"] = q_norm_g.reshape(1, Q_LORA)
    wp["kv_g"] = kv_norm_g.reshape(1, KV_LORA)
    wp["b_f"] = jnp.pad(b_forget, (0, LANES - H_F)).reshape(1, LANES)
    wp["w_g"] = wg.astype(BF16)
    wp["b_g"] = b_gate.reshape(1, 3 * D_MODEL)
    wp["w_br_a"] = W_br_a.astype(BF16)
    wp["w_br_f"] = W_br_f.astype(BF16)
    wp["w_br_m"] = W_br_m.astype(BF16)
    wp["w_out"] = W_out.astype(BF16)
    wp["ln1_g"] = ln1_g.reshape(1, D_MODEL)
    wp["ln1_b"] = ln1_b.reshape(1, D_MODEL)
    wp["w_gate"] = w_gate.astype(BF16)
    wp["w_up"] = w_up.astype(BF16)
    wp["w_down"] = w_down.astype(BF16)
    wp["ln2_g"] = ln2_g.reshape(1, D_MODEL)
    wp["ln2_b"] = ln2_b.reshape(1, D_MODEL)
    return wp


def kernel(x_prompt, x_sample, cache_mla_latent, cache_mla_krope, cache_fox_k, cache_fox_v, cache_fox_logf, cache_mem_k, cache_mem_v, page_table, mem_prompt, W_in, b_forget, b_gate, q_norm_g, W_uq, kv_norm_g, W_uk, W_uv, W_mem_k, W_mem_v, W_br_a, W_br_f, W_br_m, W_out, ln1_g, ln1_b, w_gate, w_up, w_down, ln2_g, ln2_b):
    wp = _prepare_weights(W_in, b_forget, b_gate, q_norm_g, W_uq, kv_norm_g, W_uk, W_uv,
                          W_br_a, W_br_f, W_br_m, W_out, ln1_g, ln1_b, w_gate, w_up, w_down, ln2_g, ln2_b)
    bp, sp, _ = x_prompt.shape
    bs, ss, _ = x_sample.shape
    n_pool, page, _ = cache_mla_latent.shape
    n_mem = mem_prompt.shape[1]
    past_len = page_table.shape[1] * page

    xp = x_prompt.reshape(bp * sp, D_MODEL)
    cos_p, sin_p = _rope_tables(jnp.arange(sp, dtype=jnp.int32))
    (lat_p, krope_p, fk_p, fv_p, logf_p, qcat_p, kcat_p, fq_p, fk16_p, fv16_p, mq_p) = _front(xp, cos_p, sin_p, wp)
    w_mem = jnp.concatenate([W_mem_k.reshape(D_MODEL, H_M * HD_M),
                             W_mem_v.reshape(D_MODEL, H_M * HD_M)], axis=1).astype(BF16)
    mem_kv = _matmul(mem_prompt.reshape(bp * n_mem, D_MODEL), w_mem)
    mem_k_p = mem_kv[:, :H_M * HD_M].reshape(bp, n_mem, H_M * HD_M)
    mem_v_p = mem_kv[:, H_M * HD_M:].reshape(bp, n_mem, H_M * HD_M)

    logf_t = jnp.transpose(logf_p.reshape(bp, sp, H_F), (0, 2, 1))
    c_row = _cumsum_lanes(logf_t.reshape(bp * H_F, sp)).reshape(bp, H_F // 2, 2, sp)
    c_col = jnp.transpose(c_row, (0, 1, 3, 2))

    o_lat_p = _mla_flash(qcat_p.reshape(bp, sp, H_A * QCAT), kcat_p.reshape(bp, sp, QCAT))
    o_f_p = _fox_flash(fq_p.reshape(bp, sp, -1), fk16_p.reshape(bp, sp, -1), fv16_p.reshape(bp, sp, -1),
                       c_col, c_row)
    o_m_p = _mem_attn(mq_p.reshape(bp, sp, -1), mem_k_p, mem_v_p)
    h_p = _merge(xp, o_lat_p.reshape(bp * sp, -1), o_f_p.reshape(bp * sp, -1), o_m_p.reshape(bp * sp, -1), wp)
    y_p = _mlp(h_p, wp)

    xs = x_sample.reshape(bs * ss, D_MODEL)
    tm_s = min(ROW_TILE, bs * ss)
    pos_s = past_len + (jnp.arange(tm_s, dtype=jnp.int32) % ss)
    cos_s, sin_s = _rope_tables(pos_s)
    (lat_s, krope_s, fk_s, fv_s, logf_s, qcat_s, _, fq_s, _, _, mq_s) = _front(xs, cos_s, sin_s, wp)
    pool_et = _pool_suffix(jnp.transpose(cache_fox_logf, (0, 2, 1)))
    o_lat_s, o_f_s = _decode_attn(
        page_table, qcat_s.reshape(bs, ss * H_A, QCAT), fq_s.reshape(bs, ss, -1),
        lat_s.reshape(bs, ss, -1), krope_s.reshape(bs, ss, -1), fk_s.reshape(bs, ss, -1),
        fv_s.reshape(bs, ss, -1), logf_s.reshape(bs, ss, -1),
        cache_mla_latent, cache_mla_krope, cache_fox_k.reshape(n_pool, page, -1),
        cache_fox_v.reshape(n_pool, page, -1), pool_et)
    o_m_s = _mem_decode(mq_s.reshape(bs, ss, -1), cache_mem_k.reshape(bs, n_mem, -1),
                        cache_mem_v.reshape(bs, n_mem, -1))
    h_s = _merge(xs, o_lat_s.reshape(bs * ss, -1), o_f_s.reshape(bs * ss, -1), o_m_s.reshape(bs * ss, -1), wp)
    y_s = _mlp(h_s, wp)

    return (y_p.reshape(bp, sp, D_MODEL), y_s.reshape(bs, ss, D_MODEL),
            lat_p.reshape(bp, sp, KV_LORA), krope_p.reshape(bp, sp, ROPE),
            fk_p.reshape(bp, sp, H_F, HD_F), fv_p.reshape(bp, sp, H_F, HD_F), logf_p.reshape(bp, sp, H_F),
            mem_k_p.reshape(bp, n_mem, H_M, HD_M), mem_v_p.reshape(bp, n_mem, H_M, HD_M),
            lat_s.reshape(bs, ss, KV_LORA), krope_s.reshape(bs, ss, ROPE),
            fk_s.reshape(bs, ss, H_F, HD_F), fv_s.reshape(bs, ss, H_F, HD_F), logf_s.reshape(bs, ss, H_F))
```

```python
import functools
import math

import jax
import jax.numpy as jnp
from jax import lax
from jax.experimental import pallas as pl
from jax.experimental.pallas import tpu as pltpu

F32 = jnp.float32
BF16 = jnp.bfloat16

H_A, NOPE, ROPE, V_A = 8, 128, 64, 128
Q_LORA, KV_LORA = 512, 256
H_F, HD_F = 8, 64
H_M, HD_M = 4, 128
D_MODEL = 1024
ROPE_THETA = 10000.0
LN_EPS = 1e-5
RMS_EPS = 1e-6
DEPTH = 1
ALPHA = (2 * DEPTH) ** 0.25
MLA_SCALE = (NOPE + ROPE) ** -0.5
FOX_SCALE = HD_F ** -0.5
MEM_SCALE = HD_M ** -0.5

LANES = 128
QCAT = KV_LORA + LANES
NEG_INF = float("-inf")

ROW_TILE = 256
ATT_TILE = 512
SOFTMAX_ROWS = 32
MLA_HEADS_PER_STEP = 2
CUMSUM_CHUNK = 256
POOL_PAGES_PER_STEP = 256
DECODE_PAGES_PER_STEP = 8


def _dot(a, b):
    return jnp.dot(a, b, preferred_element_type=F32)


def _dot_nt(a, b):
    return lax.dot_general(a, b, (((1,), (1,)), ((), ())), preferred_element_type=F32)


def _const_spec(shape):
    zeros = (0,) * len(shape)
    return pl.BlockSpec(shape, lambda *_: zeros, pipeline_mode=pl.Buffered(1))


def _split3(x):
    hi = x.astype(BF16).astype(F32)
    r = x - hi
    mid = r.astype(BF16).astype(F32)
    lo = (r - mid).astype(BF16).astype(F32)
    return hi, mid, lo


def _exact_dot_ones(x, tri):
    hi, mid, lo = _split3(x)
    return _dot(hi, tri) + _dot(mid, tri) + _dot(lo, tri)


def _layernorm(v, g, b):
    mu = jnp.mean(v, axis=-1, keepdims=True)
    d = v - mu
    var = jnp.mean(d * d, axis=-1, keepdims=True)
    return d * lax.rsqrt(var + LN_EPS) * g + b


def _rmsnorm(v, g):
    return v * lax.rsqrt(jnp.mean(v * v, axis=-1, keepdims=True) + RMS_EPS) * g


def _sigmoid(v):
    return 1.0 / (1.0 + jnp.exp(-v))


def _rope_tile(x, cos, sin_signed):
    lane = lax.broadcasted_iota(jnp.int32, x.shape, 1)
    rot = jnp.where(lane < ROPE // 2, pltpu.roll(x, LANES - ROPE // 2, 1), pltpu.roll(x, ROPE // 2, 1))
    return x * cos + rot * sin_signed


def _front_kernel(x_ref, cos_ref, sin_ref, w_ref, wuq_ref, wuk_ref, qg_ref, kvg_ref, bf_ref,
                  lat_ref, krope_ref, fk_ref, fv_ref, logf_ref,
                  qcat_ref, kcat_ref, fq_ref, fk16_ref, fv16_ref, mq_ref):
    xb = x_ref[...].astype(BF16)
    z = _dot(xb, w_ref[...])
    cos = cos_ref[...]
    sin = sin_ref[...]
    qn = _rmsnorm(z[:, 0:512], qg_ref[...]).astype(BF16)
    q = _dot(qn, wuq_ref[...])
    for h in range(H_A):
        q_nope = q[:, h * NOPE:(h + 1) * NOPE].astype(BF16)
        qcat_ref[:, h * QCAT:h * QCAT + KV_LORA] = _dot(q_nope, wuk_ref[h]).astype(BF16)
        xr = q[:, H_A * NOPE + h * LANES:H_A * NOPE + (h + 1) * LANES]
        qcat_ref[:, h * QCAT + KV_LORA:(h + 1) * QCAT] = _rope_tile(xr, cos, sin).astype(BF16)
    lat = _rmsnorm(z[:, 512:768], kvg_ref[...])
    lat_ref[...] = lat
    kcat_ref[:, 0:KV_LORA] = lat.astype(BF16)
    kr = _rope_tile(z[:, 2816:2944], cos, sin)
    krope_ref[...] = kr[:, 0:ROPE]
    kcat_ref[:, KV_LORA:QCAT] = kr.astype(BF16)
    fq_ref[...] = (z[:, 768:1280] * FOX_SCALE).astype(BF16)
    zfk = z[:, 1280:1792]
    zfv = z[:, 1792:2304]
    fk_ref[...] = zfk
    fv_ref[...] = zfv
    fk16_ref[...] = zfk.astype(BF16)
    fv16_ref[...] = zfv.astype(BF16)
    mq_ref[...] = z[:, 2304:2816].astype(BF16)
    zf = z[:, 2944:3072] + bf_ref[...]
    logf = jnp.minimum(zf, 0.0) - jnp.log1p(jnp.exp(-jnp.abs(zf)))
    logf_ref[...] = logf[:, 0:H_F]


def _front(x2d, cos_t, sin_t, wp):
    t = x2d.shape[0]
    tm = min(ROW_TILE, t)
    nblk = t // tm
    ntab = cos_t.shape[0] // tm
    row = lambda n: pl.BlockSpec((tm, n), lambda i: (i, 0))
    tab = pl.BlockSpec((tm, LANES), lambda i: (i % ntab, 0))
    out_shape = (
        jax.ShapeDtypeStruct((t, KV_LORA), F32), jax.ShapeDtypeStruct((t, ROPE), F32),
        jax.ShapeDtypeStruct((t, H_F * HD_F), F32), jax.ShapeDtypeStruct((t, H_F * HD_F), F32),
        jax.ShapeDtypeStruct((t, H_F), F32),
        jax.ShapeDtypeStruct((t, H_A * QCAT), BF16), jax.ShapeDtypeStruct((t, QCAT), BF16),
        jax.ShapeDtypeStruct((t, H_F * HD_F), BF16), jax.ShapeDtypeStruct((t, H_F * HD_F), BF16),
        jax.ShapeDtypeStruct((t, H_F * HD_F), BF16), jax.ShapeDtypeStruct((t, H_M * HD_M), BF16),
    )
    out_specs = (row(KV_LORA), row(ROPE), row(512), row(512), row(H_F),
                 row(H_A * QCAT), row(QCAT), row(512), row(512), row(512), row(512))
    return pl.pallas_call(
        _front_kernel,
        grid=(nblk,),
        in_specs=[row(D_MODEL), tab, tab,
                  _const_spec(wp["w_front"].shape), _const_spec(wp["w_uq"].shape),
                  _const_spec(wp["w_uk"].shape), _const_spec((1, Q_LORA)),
                  _const_spec((1, KV_LORA)), _const_spec((1, LANES))],
        out_specs=out_specs,
        out_shape=out_shape,
        compiler_params=pltpu.CompilerParams(dimension_semantics=("parallel",)),
        name="front",
    )(x2d, cos_t, sin_t, wp["w_front"], wp["w_uq"], wp["w_uk"], wp["q_g"], wp["kv_g"], wp["b_f"])


def _mm_kernel(x_ref, w_ref, o_ref):
    o_ref[...] = _dot(x_ref[...].astype(BF16), w_ref[...])


def _matmul(x2d, w):
    t, k = x2d.shape
    n = w.shape[1]
    tm = min(ROW_TILE, t)
    return pl.pallas_call(
        _mm_kernel,
        grid=(t // tm,),
        in_specs=[pl.BlockSpec((tm, k), lambda i: (i, 0)), _const_spec((k, n))],
        out_specs=pl.BlockSpec((tm, n), lambda i: (i, 0)),
        out_shape=jax.ShapeDtypeStruct((t, n), F32),
        compiler_params=pltpu.CompilerParams(dimension_semantics=("parallel",)),
        name="mem_kv_proj",
    )(x2d, w)


def _cumsum_kernel(x_ref, o_ref):
    n = x_ref.shape[1]
    ck = min(CUMSUM_CHUNK, n)
    r = lax.broadcasted_iota(jnp.int32, (ck, ck), 0)
    c = lax.broadcasted_iota(jnp.int32, (ck, ck), 1)
    tri = (r <= c).astype(F32)
    carry = jnp.zeros((x_ref.shape[0], 1), F32)
    for j in range(n // ck):
        cs = _exact_dot_ones(x_ref[:, j * ck:(j + 1) * ck], tri) + carry
        o_ref[:, j * ck:(j + 1) * ck] = cs
        carry = cs[:, ck - 1:ck]


def _cumsum_lanes(x2d):
    return pl.pallas_call(
        _cumsum_kernel,
        out_shape=jax.ShapeDtypeStruct(x2d.shape, F32),
        name="logf_cumsum",
    )(x2d)


def _lane_tile(x, width):
    n = width // LANES
    return x if n == 1 else jnp.concatenate([x] * n, axis=1)


def _softmax_rows(s_ref, p_ref, m_ref, l_ref, a_ref, idx, logit_fn, mask_pos):
    tq, tk = s_ref.shape[1:]
    rc = min(SOFTMAX_ROWS, tq)
    for c in range(tq // rc):
        rows = slice(c * rc, (c + 1) * rc)
        s = logit_fn(s_ref[idx, rows, :], rows)
        if mask_pos is not None:
            qpos = mask_pos[0] + c * rc + lax.broadcasted_iota(jnp.int32, (rc, tk), 0)
            kpos = mask_pos[1] + lax.broadcasted_iota(jnp.int32, (rc, tk), 1)
            s = jnp.where(kpos <= qpos, s, NEG_INF)
        m_prev = m_ref[idx, rows, :]
        m_new = jnp.maximum(m_prev, jnp.max(s, axis=-1, keepdims=True))
        alpha = jnp.exp(m_prev - m_new)
        p = jnp.exp(s - _lane_tile(m_new, tk))
        l_ref[idx, rows, :] = alpha * l_ref[idx, rows, :] + jnp.sum(p, axis=-1, keepdims=True)
        m_ref[idx, rows, :] = m_new
        a_ref[idx, rows, :] = alpha
        p_ref[idx, rows, :] = p.astype(BF16)


def _flash_init(ki, m_ref, l_ref, acc_ref):
    @pl.when(ki == 0)
    def _():
        m_ref[...] = jnp.full(m_ref.shape, NEG_INF, F32)
        l_ref[...] = jnp.zeros(l_ref.shape, F32)
        acc_ref[...] = jnp.zeros(acc_ref.shape, F32)


def _mla_flash_kernel(q_ref, k_ref, o_ref, s_ref, p_ref, m_ref, l_ref, a_ref, acc_ref):
    qi = pl.program_id(2)
    ki = pl.program_id(3)
    tq = q_ref.shape[1]
    tk = k_ref.shape[1]
    _flash_init(ki, m_ref, l_ref, acc_ref)

    def step(masked):
        k = k_ref[0]
        for g in range(MLA_HEADS_PER_STEP):
            s_ref[g] = _dot_nt(q_ref[0, :, g * QCAT:(g + 1) * QCAT], k)
        for g in range(MLA_HEADS_PER_STEP):
            _softmax_rows(s_ref, p_ref, m_ref, l_ref, a_ref, g, lambda s, rows: s * MLA_SCALE,
                          (qi * tq, ki * tk) if masked else None)
            acc_ref[g] = _lane_tile(a_ref[g], KV_LORA) * acc_ref[g] + _dot(p_ref[g], k[:, 0:KV_LORA])

    @pl.when(ki < qi)
    def _():
        step(False)

    @pl.when(ki == qi)
    def _():
        step(True)
        for g in range(MLA_HEADS_PER_STEP):
            o_ref[0, :, g * KV_LORA:(g + 1) * KV_LORA] = (
                acc_ref[g] / _lane_tile(l_ref[g], KV_LORA)).astype(o_ref.dtype)


def _mla_flash(qcat, kcat):
    b, s, _ = qcat.shape
    t = min(ATT_TILE, s)
    n = s // t
    hg = MLA_HEADS_PER_STEP
    return pl.pallas_call(
        _mla_flash_kernel,
        grid=(b, H_A // hg, n, n),
        in_specs=[pl.BlockSpec((1, t, hg * QCAT), lambda b, h, qi, ki: (b, qi, h)),
                  pl.BlockSpec((1, t, QCAT), lambda b, h, qi, ki: (b, jnp.minimum(ki, qi), 0))],
        out_specs=pl.BlockSpec((1, t, hg * KV_LORA), lambda b, h, qi, ki: (b, qi, h)),
        out_shape=jax.ShapeDtypeStruct((b, s, H_A * KV_LORA), BF16),
        scratch_shapes=[pltpu.VMEM((hg, t, t), F32), pltpu.VMEM((hg, t, t), BF16),
                        pltpu.VMEM((hg, t, LANES), F32), pltpu.VMEM((hg, t, LANES), F32),
                        pltpu.VMEM((hg, t, LANES), F32), pltpu.VMEM((hg, t, KV_LORA), F32)],
        compiler_params=pltpu.CompilerParams(
            dimension_semantics=("parallel", "parallel", "parallel", "arbitrary")),
        name="mla_prompt_attn",
    )(qcat, kcat)


def _fox_flash_kernel(q_ref, k_ref, v_ref, cq_ref, ck_ref, o_ref,
                      s_ref, p_ref, m_ref, l_ref, a_ref, acc_ref):
    qi = pl.program_id(2)
    ki = pl.program_id(3)
    tq = q_ref.shape[1]
    tk = k_ref.shape[1]
    _flash_init(ki, m_ref, l_ref, acc_ref)

    def step(masked):
        q2 = q_ref[0]
        k2 = k_ref[0]
        v2 = v_ref[0]
        lane = lax.broadcasted_iota(jnp.int32, q2.shape, 1)
        for hh in range(2):
            keep = (lane >= HD_F) if hh else (lane < HD_F)
            s_ref[hh] = _dot_nt(jnp.where(keep, q2, jnp.zeros_like(q2)), k2)
        for hh in range(2):
            def logits(s, rows, hh=hh):
                return s + _lane_tile(cq_ref[0, hh, rows, :], tk) - ck_ref[0, 0, hh:hh + 1, :]
            _softmax_rows(s_ref, p_ref, m_ref, l_ref, a_ref, hh, logits,
                          (qi * tq, ki * tk) if masked else None)
            acc_ref[hh] = a_ref[hh] * acc_ref[hh] + _dot(p_ref[hh], v2)

    @pl.when(ki < qi)
    def _():
        step(False)

    @pl.when(ki == qi)
    def _():
        step(True)
        lane = lax.broadcasted_iota(jnp.int32, acc_ref.shape[1:], 1)
        o0 = acc_ref[0] / l_ref[0]
        o1 = acc_ref[1] / l_ref[1]
        o_ref[0] = jnp.where(lane < HD_F, o0, o1).astype(o_ref.dtype)


def _fox_flash(fq, fk16, fv16, c_col, c_row):
    b, s, _ = fq.shape
    t = min(ATT_TILE, s)
    n = s // t
    npair = H_F // 2
    qspec = pl.BlockSpec((1, t, LANES), lambda b, h, qi, ki: (b, qi, h))
    kspec = pl.BlockSpec((1, t, LANES), lambda b, h, qi, ki: (b, jnp.minimum(ki, qi), h))
    return pl.pallas_call(
        _fox_flash_kernel,
        grid=(b, npair, n, n),
        in_specs=[qspec, kspec, kspec,
                  pl.BlockSpec((1, 2, t, LANES), lambda b, h, qi, ki: (b, h, qi, 0)),
                  pl.BlockSpec((1, 1, 2, t), lambda b, h, qi, ki: (b, h, 0, jnp.minimum(ki, qi)))],
        out_specs=qspec,
        out_shape=jax.ShapeDtypeStruct((b, s, H_F * HD_F), BF16),
        scratch_shapes=[pltpu.VMEM((2, t, t), F32), pltpu.VMEM((2, t, t), BF16),
                        pltpu.VMEM((2, t, LANES), F32), pltpu.VMEM((2, t, LANES), F32),
                        pltpu.VMEM((2, t, LANES), F32), pltpu.VMEM((2, t, LANES), F32)],
        compiler_params=pltpu.CompilerParams(
            dimension_semantics=("parallel", "parallel", "parallel", "arbitrary")),
        name="fox_prompt_attn",
    )(fq, fk16, fv16, c_col, c_row)


def _mem_attn_kernel(q_ref, k_ref, v_ref, o_ref):
    s = _dot_nt(q_ref[0], k_ref[0].astype(BF16)) * MEM_SCALE
    m = jnp.max(s, axis=-1, keepdims=True)
    p = jnp.exp(s - m)
    l = jnp.sum(p, axis=-1, keepdims=True)
    o_ref[0] = (_dot(p.astype(BF16), v_ref[0].astype(BF16)) / l).astype(o_ref.dtype)


def _mem_attn(mq, mem_k, mem_v):
    b, s, _ = mq.shape
    nm = mem_k.shape[1]
    t = min(ATT_TILE, s)
    return pl.pallas_call(
        _mem_attn_kernel,
        grid=(b, H_M, s // t),
        in_specs=[pl.BlockSpec((1, t, HD_M), lambda b, h, qi: (b, qi, h)),
                  pl.BlockSpec((1, nm, HD_M), lambda b, h, qi: (b, 0, h)),
                  pl.BlockSpec((1, nm, HD_M), lambda b, h, qi: (b, 0, h))],
        out_specs=pl.BlockSpec((1, t, HD_M), lambda b, h, qi: (b, qi, h)),
        out_shape=jax.ShapeDtypeStruct((b, s, H_M * HD_M), BF16),
        compiler_params=pltpu.CompilerParams(
            dimension_semantics=("parallel", "parallel", "parallel")),
        name="mem_prompt_attn",
    )(mq, mem_k, mem_v)


def _merge_kernel(x_ref, olat_ref, of_ref, om_ref, wg_ref, bg_ref, wuv_ref, wa_ref, wf_ref, wm_ref,
                  wo_ref, g_ref, b_ref, h_ref, oa_ref):
    x = x_ref[...]
    xb = x.astype(BF16)
    for h in range(H_A):
        oa_ref[:, h * V_A:(h + 1) * V_A] = _dot(
            olat_ref[:, h * KV_LORA:(h + 1) * KV_LORA], wuv_ref[h]).astype(BF16)
    branches = (_dot(oa_ref[...], wa_ref[...]), _dot(of_ref[...], wf_ref[...]),
                _dot(om_ref[...], wm_ref[...]))
    merged = None
    for i, br in enumerate(branches):
        gate = _sigmoid(_dot(xb, wg_ref[:, i * D_MODEL:(i + 1) * D_MODEL])
                        + bg_ref[:, i * D_MODEL:(i + 1) * D_MODEL])
        merged = gate * br if merged is None else merged + gate * br
    pre = ALPHA * x + _dot(merged.astype(BF16), wo_ref[...])
    h_ref[...] = _layernorm(pre, g_ref[...], b_ref[...])


def _merge(x2d, o_lat, o_f, o_m, wp):
    t = x2d.shape[0]
    tm = min(ROW_TILE, t)
    row = lambda n: pl.BlockSpec((tm, n), lambda i: (i, 0))
    names = ("w_g", "b_g", "w_uv", "w_br_a", "w_br_f", "w_br_m", "w_out", "ln1_g", "ln1_b")
    return pl.pallas_call(
        _merge_kernel,
        grid=(t // tm,),
        in_specs=[row(D_MODEL), row(H_A * KV_LORA), row(H_F * HD_F), row(H_M * HD_M)]
        + [_const_spec(wp[n].shape) for n in names],
        out_specs=row(D_MODEL),
        out_shape=jax.ShapeDtypeStruct((t, D_MODEL), F32),
        scratch_shapes=[pltpu.VMEM((tm, H_A * V_A), BF16)],
        compiler_params=pltpu.CompilerParams(dimension_semantics=("parallel",)),
        name="merge_ln1",
    )(x2d, o_lat, o_f, o_m, *[wp[n] for n in names])


def _mlp_kernel(h_ref, wg_ref, wu_ref, wd_ref, g_ref, b_ref, y_ref, a_ref, *, n_chunks):
    h = h_ref[...]
    hb = h.astype(BF16)
    fc = wg_ref.shape[1] // n_chunks
    for c in range(n_chunks):
        gt = _dot(hb, wg_ref[:, c * fc:(c + 1) * fc])
        up = _dot(hb, wu_ref[:, c * fc:(c + 1) * fc])
        a_ref[:, c * fc:(c + 1) * fc] = (gt * _sigmoid(gt) * up).astype(BF16)
    y_ref[...] = _layernorm(ALPHA * h + _dot(a_ref[...], wd_ref[...]), g_ref[...], b_ref[...])


def _mlp(h2d, wp):
    t = h2d.shape[0]
    tm = min(ROW_TILE, t)
    d_ff = wp["w_gate"].shape[1]
    n_chunks = 2
    row = lambda n: pl.BlockSpec((tm, n), lambda i: (i, 0))
    names = ("w_gate", "w_up", "w_down", "ln2_g", "ln2_b")
    return pl.pallas_call(
        functools.partial(_mlp_kernel, n_chunks=n_chunks),
        grid=(t // tm,),
        in_specs=[row(D_MODEL)] + [_const_spec(wp[n].shape) for n in names],
        out_specs=row(D_MODEL),
        out_shape=jax.ShapeDtypeStruct((t, D_MODEL), F32),
        scratch_shapes=[pltpu.VMEM((tm, d_ff), BF16)],
        compiler_params=pltpu.CompilerParams(dimension_semantics=("parallel",)),
        name="mlp_ln2",
    )(h2d, *[wp[n] for n in names])


def _pool_suffix_kernel(x_ref, o_ref):
    pb = x_ref.shape[0]
    l = x_ref[...].reshape(pb * H_F, LANES)
    r = lax.broadcasted_iota(jnp.int32, (LANES, LANES), 0)
    c = lax.broadcasted_iota(jnp.int32, (LANES, LANES), 1)
    excl = _exact_dot_ones(l, (r > c).astype(F32))
    tot = jnp.broadcast_to(excl[:, 0:1] + l[:, 0:1], excl.shape)
    o_ref[:, 0:H_F, :] = excl.reshape(pb, H_F, LANES)
    o_ref[:, H_F:2 * H_F, :] = tot.reshape(pb, H_F, LANES)


def _pool_suffix(logf_t):
    n_pool = logf_t.shape[0]
    pb = min(POOL_PAGES_PER_STEP, n_pool)
    return pl.pallas_call(
        _pool_suffix_kernel,
        grid=(n_pool // pb,),
        in_specs=[pl.BlockSpec((pb, H_F, LANES), lambda i: (i, 0, 0))],
        out_specs=pl.BlockSpec((pb, 2 * H_F, LANES), lambda i: (i, 0, 0)),
        out_shape=jax.ShapeDtypeStruct((n_pool, 2 * H_F, LANES), F32),
        compiler_params=pltpu.CompilerParams(dimension_semantics=("parallel",)),
        name="pool_logf_suffix",
    )(logf_t)


def _rows_per_query(x, nq):
    return jnp.concatenate([jnp.broadcast_to(x[q:q + 1], (8, x.shape[1])) for q in range(nq)], axis=0)


def _decode_kernel(pt_ref, qcat_ref, fq_ref, latn_ref, krn_ref, fkn_ref, fvn_ref, lfn_ref, *rest,
                   ch, nq, nch):
    del pt_ref
    lat_v, kr_v, fk_v, fv_v, et_v = (rest[i * ch:(i + 1) * ch] for i in range(5))
    olat_ref, of_ref = rest[5 * ch:5 * ch + 2]
    (qbd_ref, cq_ref, carry_ref, m_ref, l_ref, acca_ref, accf_ref,
     latc_ref, krc_ref, fkc_ref, fvc_ref) = rest[5 * ch + 2:]
    j = pl.program_id(1)
    nrow = nq * 8
    page = lat_v[0].shape[1]

    row512 = lax.broadcasted_iota(jnp.int32, (nrow, H_F * HD_F), 0)
    lane512 = lax.broadcasted_iota(jnp.int32, (nrow, H_F * HD_F), 1)
    head_mask = (lane512 // HD_F) == (row512 % 8)

    @pl.when(j == 0)
    def _init():
        carry_ref[...] = jnp.zeros(carry_ref.shape, F32)
        qrow = lax.broadcasted_iota(jnp.int32, (nrow, 1), 0) // 8
        qbd = jnp.where(head_mask, _rows_per_query(fq_ref[0].astype(F32), nq), 0.0)
        qbd_ref[...] = qbd.astype(BF16)
        lfn = lfn_ref[0]
        r8 = lax.broadcasted_iota(jnp.int32, (nrow, H_F), 0)
        l8 = lax.broadcasted_iota(jnp.int32, (nrow, H_F), 1)
        pick = l8 == (r8 % 8)
        cum = []
        for q in range(nq):
            cum.append(lfn[q:q + 1] if q == 0 else cum[-1] + lfn[q:q + 1])
        cq = jnp.sum(jnp.where(pick, jnp.concatenate(
            [jnp.broadcast_to(c, (8, H_F)) for c in cum], axis=0), 0.0), axis=1, keepdims=True)
        cq_ref[...] = cq
        fkn = fkn_ref[0]
        fvn = fvn_ref[0]
        sf = []
        for s in range(nq):
            ck = jnp.sum(jnp.where(pick, jnp.broadcast_to(cum[s], (nrow, H_F)), 0.0),
                         axis=1, keepdims=True)
            v = jnp.sum(qbd * fkn[s:s + 1], axis=1, keepdims=True) + cq - ck
            sf.append(jnp.where(qrow >= s, v, NEG_INF))
        qc = qcat_ref[0].astype(F32)
        latn = latn_ref[0]
        krn = krn_ref[0]
        sa = []
        for s in range(nq):
            v = (jnp.sum(qc[:, 0:KV_LORA] * latn[s:s + 1], axis=1, keepdims=True)
                 + jnp.sum(qc[:, KV_LORA:KV_LORA + ROPE] * krn[s:s + 1], axis=1, keepdims=True))
            sa.append(jnp.where(qrow >= s, v * MLA_SCALE, NEG_INF))
        for idx, (sc, vals, acc_ref) in enumerate(((sa, latn, acca_ref), (sf, fvn, accf_ref))):
            m = sc[0]
            for s in range(1, nq):
                m = jnp.maximum(m, sc[s])
            l = jnp.zeros_like(m)
            acc = jnp.zeros(acc_ref.shape, F32)
            for s in range(nq):
                p = jnp.exp(sc[s] - m)
                l = l + p
                acc = acc + p * vals[s:s + 1]
            m_ref[idx] = m
            l_ref[idx] = l
            acc_ref[...] = acc

    for i in range(ch):
        sl = slice(i * page, (i + 1) * page)
        latc_ref[sl, :] = lat_v[i][0].astype(BF16)
        krc_ref[:, sl] = kr_v[i][0].astype(BF16)
        fkc_ref[:, sl] = fk_v[i][0].astype(BF16)
        fvc_ref[:, sl] = fv_v[i][0].astype(BF16)

    carry = carry_ref[...]
    pieces = [None] * ch
    for i in reversed(range(ch)):
        et = et_v[i][0]
        pieces[i] = et[0:H_F] + carry
        carry = carry + et[H_F:2 * H_F]
    carry_ref[...] = carry
    bias = jnp.concatenate(pieces, axis=1)
    bias = jnp.concatenate([bias] * nq, axis=0)

    def update(s, pv_fn, acc_ref, idx):
        m_prev = m_ref[idx]
        m_new = jnp.maximum(m_prev, jnp.max(s, axis=-1, keepdims=True))
        alpha = jnp.exp(m_prev - m_new)
        p = jnp.exp(s - m_new)
        l_ref[idx] = alpha * l_ref[idx] + jnp.sum(p, axis=-1, keepdims=True)
        acc_ref[...] = alpha * acc_ref[...] + pv_fn(p.astype(BF16))
        m_ref[idx] = m_new

    sf = _dot(qbd_ref[...], fkc_ref[...]) + bias + cq_ref[...]
    update(sf, lambda p: _dot_nt(p, fvc_ref[...]), accf_ref, 1)
    qc = qcat_ref[0]
    latc = latc_ref[...]
    sa = (_dot_nt(qc[:, 0:KV_LORA], latc)
          + _dot(qc[:, KV_LORA:KV_LORA + ROPE], krc_ref[...])) * MLA_SCALE
    update(sa, lambda p: _dot(p, latc), acca_ref, 0)

    @pl.when(j == nch - 1)
    def _fin():
        olat_ref[0] = (acca_ref[...] / l_ref[0]).astype(olat_ref.dtype)
        of = jnp.where(head_mask, accf_ref[...] / l_ref[1], 0.0)
        of_ref[0] = jnp.sum(of.reshape(nq, 8, H_F * HD_F), axis=1).astype(of_ref.dtype)


def _decode_attn(page_table, qcat_s, fq_s, lat_s, krope_s, fk_s, fv_s, logf_s,
                 pool_lat, pool_kr, pool_fk, pool_fv, pool_et):
    nb, npages = page_table.shape
    nq = fq_s.shape[1]
    page = pool_lat.shape[1]
    ch = min(DECODE_PAGES_PER_STEP, npages)
    nch = npages // ch
    nrow = nq * 8
    pt_flat = page_table.reshape(-1)

    def per_b(shape):
        return pl.BlockSpec((1,) + shape, lambda b, j, pt: (b,) + (0,) * len(shape))

    def page_specs(arr):
        blk = (1,) + arr.shape[1:]
        zeros = (0,) * (arr.ndim - 1)
        return [pl.BlockSpec(blk, lambda b, j, pt, i=i: (pt[b * npages + (nch - 1 - j) * ch + i],) + zeros)
                for i in range(ch)]

    pools = (pool_lat, pool_kr, pool_fk, pool_fv, pool_et)
    in_specs = [per_b((nrow, QCAT)), per_b((nq, H_F * HD_F)), per_b((nq, KV_LORA)), per_b((nq, ROPE)),
                per_b((nq, H_F * HD_F)), per_b((nq, H_F * HD_F)), per_b((nq, H_F))]
    args = [qcat_s, fq_s, lat_s, krope_s, fk_s, fv_s, logf_s]
    for arr in pools:
        in_specs += page_specs(arr)
        args += [arr] * ch
    grid_spec = pltpu.PrefetchScalarGridSpec(
        num_scalar_prefetch=1,
        grid=(nb, nch),
        in_specs=in_specs,
        out_specs=(per_b((nrow, KV_LORA)), per_b((nq, H_F * HD_F))),
        scratch_shapes=[
            pltpu.VMEM((nrow, H_F * HD_F), BF16), pltpu.VMEM((nrow, 1), F32),
            pltpu.VMEM((H_F, LANES), F32),
            pltpu.VMEM((2, nrow, 1), F32), pltpu.VMEM((2, nrow, 1), F32),
            pltpu.VMEM((nrow, KV_LORA), F32), pltpu.VMEM((nrow, H_F * HD_F), F32),
            pltpu.VMEM((ch * page, KV_LORA), BF16), pltpu.VMEM((ROPE, ch * page), BF16),
            pltpu.VMEM((H_F * HD_F, ch * page), BF16), pltpu.VMEM((H_F * HD_F, ch * page), BF16),
        ],
    )
    return pl.pallas_call(
        functools.partial(_decode_kernel, ch=ch, nq=nq, nch=nch),
        grid_spec=grid_spec,
        out_shape=(jax.ShapeDtypeStruct((nb, nrow, KV_LORA), BF16),
                   jax.ShapeDtypeStruct((nb, nq, H_F * HD_F), BF16)),
        compiler_params=pltpu.CompilerParams(dimension_semantics=("parallel", "arbitrary")),
        name="decode_attn",
    )(pt_flat, *args)


def _mem_decode_kernel(q_ref, k_ref, v_ref, o_ref, *, nq):
    nrow = nq * 8
    row = lax.broadcasted_iota(jnp.int32, (nrow, H_M * HD_M), 0)
    lane = lax.broadcasted_iota(jnp.int32, (nrow, H_M * HD_M), 1)
    head_mask = (lane // HD_M) == (row % 8)
    qbd = jnp.where(head_mask, _rows_per_query(q_ref[0].astype(F32), nq), 0.0).astype(BF16)
    s = _dot_nt(qbd, k_ref[0].astype(BF16)) * MEM_SCALE
    m = jnp.max(s, axis=-1, keepdims=True)
    p = jnp.exp(s - m)
    l = jnp.sum(p, axis=-1, keepdims=True)
    o = jnp.where(head_mask, _dot(p.astype(BF16), v_ref[0].astype(BF16)) / l, 0.0)
    o_ref[0] = jnp.sum(o.reshape(nq, 8, H_M * HD_M), axis=1).astype(o_ref.dtype)


def _mem_decode(mq_s, mem_k, mem_v):
    nb, nq, d = mq_s.shape
    nm = mem_k.shape[1]
    return pl.pallas_call(
        functools.partial(_mem_decode_kernel, nq=nq),
        grid=(nb,),
        in_specs=[pl.BlockSpec((1, nq, d), lambda b: (b, 0, 0)),
                  pl.BlockSpec((1, nm, d), lambda b: (b, 0, 0)),
                  pl.BlockSpec((1, nm, d), lambda b: (b, 0, 0))],
        out_specs=pl.BlockSpec((1, nq, d), lambda b: (b, 0, 0)),
        out_shape=jax.ShapeDtypeStruct((nb, nq, d), BF16),
        compiler_params=pltpu.CompilerParams(dimension_semantics=("parallel",)),
        name="mem_decode_attn",
    )(mq_s, mem_k, mem_v)


def _rope_tables(pos):
    half = ROPE // 2
    inv = jnp.exp(-math.log(ROPE_THETA) * jnp.arange(half, dtype=F32) / half)
    ang = pos.astype(F32)[:, None] * inv
    cos, sin = jnp.cos(ang), jnp.sin(ang)
    pad = jnp.zeros((pos.shape[0], LANES - ROPE), F32)
    return (jnp.concatenate([cos, cos, pad], axis=1), jnp.concatenate([-sin, sin, pad], axis=1))


def _prepare_weights(W_in, b_forget, b_gate, q_norm_g, W_uq, kv_norm_g, W_uk, W_uv,
                     W_br_a, W_br_f, W_br_m, W_out, ln1_g, ln1_b, w_gate, w_up, w_down, ln2_g, ln2_b):
    splits = (Q_LORA, KV_LORA, ROPE, H_F * HD_F, H_F * HD_F, H_F * HD_F, H_F, H_M * HD_M, 3 * D_MODEL)
    offs = [0]
    for n in splits:
        offs.append(offs[-1] + n)
    wq, wkv, wkr, wfq, wfk, wfv, wf, wmq, wg = (W_in[:, offs[i]:offs[i + 1]] for i in range(9))
    pad_to = lambda w: jnp.pad(w, ((0, 0), (0, LANES - w.shape[1])))
    wp = {}
    wp["w_front"] = jnp.concatenate([wq, wkv, wfq, wfk, wfv, wmq, pad_to(wkr), pad_to(wf)], axis=1).astype(BF16)
    w_nope = W_uq[:, :, :NOPE].reshape(Q_LORA, H_A * NOPE)
    w_rope = jnp.pad(W_uq[:, :, NOPE:], ((0, 0), (0, 0), (0, LANES - ROPE))).reshape(Q_LORA, H_A * LANES)
    wp["w_uq"] = jnp.concatenate([w_nope, w_rope], axis=1).astype(BF16)
    wp["w_uk"] = jnp.transpose(W_uk, (1, 2, 0)).astype(BF16)
    wp["w_uv"] = jnp.transpose(W_uv, (1, 0, 2)).astype(BF16)
    wp["q_g"] = q_norm_g.reshape(1, Q_LORA)
    wp["kv_g"] = kv_norm_g.reshape(1, KV_LORA)
    wp["b_f"] = jnp.pad(b_forget, (0, LANES - H_F)).reshape(1, LANES)
    wp["w_g"] = wg.astype(BF16)
    wp["b_g"] = b_gate.reshape(1, 3 * D_MODEL)
    wp["w_br_a"] = W_br_a.astype(BF16)
    wp["w_br_f"] = W_br_f.astype(BF16)
    wp["w_br_m"] = W_br_m.astype(BF16)
    wp["w_out"] = W_out.astype(BF16)
    wp["ln1_g"] = ln1_g.reshape(1, D_MODEL)
    wp["ln1_b"] = ln1_b.reshape(1, D_MODEL)
    wp["w_gate"] = w_gate.astype(BF16)
    wp["w_up"] = w_up.astype(BF16)
    wp["w_down"] = w_down.astype(BF16)
    wp["ln2_g"] = ln2_g.reshape(1, D_MODEL)
    wp["ln2_b"] = ln2_b.reshape(1, D_MODEL)
    return wp


def kernel(x_prompt, x_sample, cache_mla_latent, cache_mla_krope, cache_fox_k, cache_fox_v, cache_fox_logf, cache_mem_k, cache_mem_v, page_table, mem_prompt, W_in, b_forget, b_gate, q_norm_g, W_uq, kv_norm_g, W_uk, W_uv, W_mem_k, W_mem_v, W_br_a, W_br_f, W_br_m, W_out, ln1_g, ln1_b, w_gate, w_up, w_down, ln2_g, ln2_b):
    wp = _prepare_weights(W_in, b_forget, b_gate, q_norm_g, W_uq, kv_norm_g, W_uk, W_uv,
                          W_br_a, W_br_f, W_br_m, W_out, ln1_g, ln1_b, w_gate, w_up, w_down, ln2_g, ln2_b)
    bp, sp, _ = x_prompt.shape
    bs, ss, _ = x_sample.shape
    n_pool, page, _ = cache_mla_latent.shape
    n_mem = mem_prompt.shape[1]
    past_len = page_table.shape[1] * page

    xp = x_prompt.reshape(bp * sp, D_MODEL)
    cos_p, sin_p = _rope_tables(jnp.arange(sp, dtype=jnp.int32))
    (lat_p, krope_p, fk_p, fv_p, logf_p, qcat_p, kcat_p, fq_p, fk16_p, fv16_p, mq_p) = _front(xp, cos_p, sin_p, wp)
    w_mem = jnp.concatenate([W_mem_k.reshape(D_MODEL, H_M * HD_M),
                             W_mem_v.reshape(D_MODEL, H_M * HD_M)], axis=1).astype(BF16)
    mem_kv = _matmul(mem_prompt.reshape(bp * n_mem, D_MODEL), w_mem)
    mem_k_p = mem_kv[:, :H_M * HD_M].reshape(bp, n_mem, H_M * HD_M)
    mem_v_p = mem_kv[:, H_M * HD_M:].reshape(bp, n_mem, H_M * HD_M)

    logf_t = jnp.transpose(logf_p.reshape(bp, sp, H_F), (0, 2, 1))
    c = _cumsum_lanes(logf_t.reshape(bp * H_F, sp))
    c_row = c.reshape(bp, H_F // 2, 2, sp)
    c_col = jnp.broadcast_to(c.reshape(bp, H_F, sp, 1), (bp, H_F, sp, LANES))

    o_lat_p = _mla_flash(qcat_p.reshape(bp, sp, H_A * QCAT), kcat_p.reshape(bp, sp, QCAT))
    o_f_p = _fox_flash(fq_p.reshape(bp, sp, -1), fk16_p.reshape(bp, sp, -1), fv16_p.reshape(bp, sp, -1),
                       c_col, c_row)
    o_m_p = _mem_attn(mq_p.reshape(bp, sp, -1), mem_k_p, mem_v_p)
    h_p = _merge(xp, o_lat_p.reshape(bp * sp, -1), o_f_p.reshape(bp * sp, -1), o_m_p.reshape(bp * sp, -1), wp)
    y_p = _mlp(h_p, wp)

    xs = x_sample.reshape(bs * ss, D_MODEL)
    tm_s = min(ROW_TILE, bs * ss)
    pos_s = past_len + (jnp.arange(tm_s, dtype=jnp.int32) % ss)
    cos_s, sin_s = _rope_tables(pos_s)
    (lat_s, krope_s, fk_s, fv_s, logf_s, qcat_s, _, fq_s, _, _, mq_s) = _front(xs, cos_s, sin_s, wp)
    pool_et = _pool_suffix(jnp.transpose(cache_fox_logf, (0, 2, 1)))
    pool_kr_t = jnp.transpose(cache_mla_krope, (0, 2, 1))
    pool_fk_t = jnp.transpose(cache_fox_k, (0, 2, 3, 1)).reshape(n_pool, H_F * HD_F, page)
    pool_fv_t = jnp.transpose(cache_fox_v, (0, 2, 3, 1)).reshape(n_pool, H_F * HD_F, page)
    o_lat_s, o_f_s = _decode_attn(
        page_table, qcat_s.reshape(bs, ss * H_A, QCAT), fq_s.reshape(bs, ss, -1),
        lat_s.reshape(bs, ss, -1), krope_s.reshape(bs, ss, -1), fk_s.reshape(bs, ss, -1),
        fv_s.reshape(bs, ss, -1), logf_s.reshape(bs, ss, -1),
        cache_mla_latent, pool_kr_t, pool_fk_t, pool_fv_t, pool_et)
    o_m_s = _mem_decode(mq_s.reshape(bs, ss, -1), cache_mem_k.reshape(bs, n_mem, -1),
                        cache_mem_v.reshape(bs, n_mem, -1))
    h_s = _merge(xs, o_lat_s.reshape(bs * ss, -1), o_f_s.reshape(bs * ss, -1), o_m_s.reshape(bs * ss, -1), wp)
    y_s = _mlp(h_s, wp)

    return (y_p.reshape(bp, sp, D_MODEL), y_s.reshape(bs, ss, D_MODEL),
            lat_p.reshape(bp, sp, KV_LORA), krope_p.reshape(bp, sp, ROPE),
            fk_p.reshape(bp, sp, H_F, HD_F), fv_p.reshape(bp, sp, H_F, HD_F), logf_p.reshape(bp, sp, H_F),
            mem_k_p.reshape(bp, n_mem, H_M, HD_M), mem_v_p.reshape(bp, n_mem, H_M, HD_M),
            lat_s.reshape(bs, ss, KV_LORA), krope_s.reshape(bs, ss, ROPE),
            fk_s.reshape(bs, ss, H_F, HD_F), fv_s.reshape(bs, ss, H_F, HD_F), logf_s.reshape(bs, ss, H_F))
```

```python
import functools
import math

import jax
import jax.numpy as jnp
from jax import lax
from jax.experimental import pallas as pl
from jax.experimental.pallas import tpu as pltpu

F32 = jnp.float32
BF16 = jnp.bfloat16

H_A, NOPE, ROPE, V_A = 8, 128, 64, 128
Q_LORA, KV_LORA = 512, 256
H_F, HD_F = 8, 64
H_M, HD_M = 4, 128
D_MODEL = 1024
ROPE_THETA = 10000.0
LN_EPS = 1e-5
RMS_EPS = 1e-6
DEPTH = 1
ALPHA = (2 * DEPTH) ** 0.25
MLA_SCALE = (NOPE + ROPE) ** -0.5
FOX_SCALE = HD_F ** -0.5
MEM_SCALE = HD_M ** -0.5

LANES = 128
QCAT = KV_LORA + LANES
NEG_INF = float("-inf")

ROW_TILE = 256
ATT_TILE = 512
SOFTMAX_ROWS = 32
MLA_HEADS_PER_STEP = 4
FOX_PAIRS_PER_STEP = 2
MEM_DECODE_SEQS_PER_STEP = 4
CUMSUM_CHUNK = 256
POOL_PAGES_PER_STEP = 256
DECODE_PAGES_PER_GROUP = 8


def _dot(a, b):
    return jnp.dot(a, b, preferred_element_type=F32)


def _dot_nt(a, b):
    return lax.dot_general(a, b, (((1,), (1,)), ((), ())), preferred_element_type=F32)


def _const_spec(shape):
    zeros = (0,) * len(shape)
    return pl.BlockSpec(shape, lambda *_: zeros, pipeline_mode=pl.Buffered(1))


def _split3(x):
    hi = x.astype(BF16).astype(F32)
    r = x - hi
    mid = r.astype(BF16).astype(F32)
    lo = (r - mid).astype(BF16).astype(F32)
    return hi, mid, lo


def _exact_dot_ones(x, tri):
    hi, mid, lo = _split3(x)
    return _dot(hi, tri) + _dot(mid, tri) + _dot(lo, tri)


def _layernorm(v, g, b):
    mu = jnp.mean(v, axis=-1, keepdims=True)
    d = v - mu
    var = jnp.mean(d * d, axis=-1, keepdims=True)
    return d * lax.rsqrt(var + LN_EPS) * g + b


def _rmsnorm(v, g):
    return v * lax.rsqrt(jnp.mean(v * v, axis=-1, keepdims=True) + RMS_EPS) * g


def _sigmoid(v):
    return 1.0 / (1.0 + jnp.exp(-v))


def _rope_tile(x, cos, sin_signed):
    lane = lax.broadcasted_iota(jnp.int32, x.shape, 1)
    rot = jnp.where(lane < ROPE // 2, pltpu.roll(x, LANES - ROPE // 2, 1), pltpu.roll(x, ROPE // 2, 1))
    return x * cos + rot * sin_signed


def _front_kernel(x_ref, cos_ref, sin_ref, w_ref, wuq_ref, wuk_ref, qg_ref, kvg_ref, bf_ref,
                  lat_ref, krope_ref, fk_ref, fv_ref, logf_ref,
                  qcat_ref, kcat_ref, fq_ref, fk16_ref, fv16_ref, mq_ref):
    xb = x_ref[...].astype(BF16)
    z = _dot(xb, w_ref[...])
    cos = cos_ref[...]
    sin = sin_ref[...]
    qn = _rmsnorm(z[:, 0:512], qg_ref[...]).astype(BF16)
    q = _dot(qn, wuq_ref[...])
    for h in range(H_A):
        q_nope = q[:, h * NOPE:(h + 1) * NOPE].astype(BF16)
        qcat_ref[:, h * QCAT:h * QCAT + KV_LORA] = _dot(q_nope, wuk_ref[h]).astype(BF16)
        xr = q[:, H_A * NOPE + h * LANES:H_A * NOPE + (h + 1) * LANES]
        qcat_ref[:, h * QCAT + KV_LORA:(h + 1) * QCAT] = _rope_tile(xr, cos, sin).astype(BF16)
    lat = _rmsnorm(z[:, 512:768], kvg_ref[...])
    lat_ref[...] = lat
    kcat_ref[:, 0:KV_LORA] = lat.astype(BF16)
    kr = _rope_tile(z[:, 2816:2944], cos, sin)
    krope_ref[...] = kr[:, 0:ROPE]
    kcat_ref[:, KV_LORA:QCAT] = kr.astype(BF16)
    fq_ref[...] = (z[:, 768:1280] * FOX_SCALE).astype(BF16)
    zfk = z[:, 1280:1792]
    zfv = z[:, 1792:2304]
    fk_ref[...] = zfk
    fv_ref[...] = zfv
    fk16_ref[...] = zfk.astype(BF16)
    fv16_ref[...] = zfv.astype(BF16)
    mq_ref[...] = z[:, 2304:2816].astype(BF16)
    zf = z[:, 2944:3072] + bf_ref[...]
    logf = jnp.minimum(zf, 0.0) - jnp.log1p(jnp.exp(-jnp.abs(zf)))
    logf_ref[...] = logf[:, 0:H_F]


def _front(x2d, cos_t, sin_t, wp):
    t = x2d.shape[0]
    tm = min(ROW_TILE, t)
    nblk = t // tm
    ntab = cos_t.shape[0] // tm
    row = lambda n: pl.BlockSpec((tm, n), lambda i: (i, 0))
    tab = pl.BlockSpec((tm, LANES), lambda i: (i % ntab, 0))
    out_shape = (
        jax.ShapeDtypeStruct((t, KV_LORA), F32), jax.ShapeDtypeStruct((t, ROPE), F32),
        jax.ShapeDtypeStruct((t, H_F * HD_F), F32), jax.ShapeDtypeStruct((t, H_F * HD_F), F32),
        jax.ShapeDtypeStruct((t, H_F), F32),
        jax.ShapeDtypeStruct((t, H_A * QCAT), BF16), jax.ShapeDtypeStruct((t, QCAT), BF16),
        jax.ShapeDtypeStruct((t, H_F * HD_F), BF16), jax.ShapeDtypeStruct((t, H_F * HD_F), BF16),
        jax.ShapeDtypeStruct((t, H_F * HD_F), BF16), jax.ShapeDtypeStruct((t, H_M * HD_M), BF16),
    )
    out_specs = (row(KV_LORA), row(ROPE), row(512), row(512), row(H_F),
                 row(H_A * QCAT), row(QCAT), row(512), row(512), row(512), row(512))
    return pl.pallas_call(
        _front_kernel,
        grid=(nblk,),
        in_specs=[row(D_MODEL), tab, tab,
                  _const_spec(wp["w_front"].shape), _const_spec(wp["w_uq"].shape),
                  _const_spec(wp["w_uk"].shape), _const_spec((1, Q_LORA)),
                  _const_spec((1, KV_LORA)), _const_spec((1, LANES))],
        out_specs=out_specs,
        out_shape=out_shape,
        compiler_params=pltpu.CompilerParams(dimension_semantics=("parallel",)),
        name="front",
    )(x2d, cos_t, sin_t, wp["w_front"], wp["w_uq"], wp["w_uk"], wp["q_g"], wp["kv_g"], wp["b_f"])


def _mm_kernel(x_ref, w_ref, o_ref):
    o_ref[...] = _dot(x_ref[...].astype(BF16), w_ref[...])


def _matmul(x2d, w):
    t, k = x2d.shape
    n = w.shape[1]
    tm = min(ROW_TILE, t)
    return pl.pallas_call(
        _mm_kernel,
        grid=(t // tm,),
        in_specs=[pl.BlockSpec((tm, k), lambda i: (i, 0)), _const_spec((k, n))],
        out_specs=pl.BlockSpec((tm, n), lambda i: (i, 0)),
        out_shape=jax.ShapeDtypeStruct((t, n), F32),
        compiler_params=pltpu.CompilerParams(dimension_semantics=("parallel",)),
        name="mem_kv_proj",
    )(x2d, w)


def _cumsum_kernel(x_ref, o_ref):
    n = x_ref.shape[1]
    ck = min(CUMSUM_CHUNK, n)
    r = lax.broadcasted_iota(jnp.int32, (ck, ck), 0)
    c = lax.broadcasted_iota(jnp.int32, (ck, ck), 1)
    tri = (r <= c).astype(F32)
    carry = jnp.zeros((x_ref.shape[0], 1), F32)
    for j in range(n // ck):
        cs = _exact_dot_ones(x_ref[:, j * ck:(j + 1) * ck], tri) + carry
        o_ref[:, j * ck:(j + 1) * ck] = cs
        carry = cs[:, ck - 1:ck]


def _cumsum_lanes(x2d):
    return pl.pallas_call(
        _cumsum_kernel,
        out_shape=jax.ShapeDtypeStruct(x2d.shape, F32),
        name="logf_cumsum",
    )(x2d)


def _lane_tile(x, width):
    n = width // LANES
    return x if n == 1 else jnp.concatenate([x] * n, axis=1)


def _softmax_rows(s_ref, p_ref, m_ref, l_ref, a_ref, idx, logit_fn, mask_pos):
    tq, tk = s_ref.shape[1:]
    rc = min(SOFTMAX_ROWS, tq)
    for c in range(tq // rc):
        rows = slice(c * rc, (c + 1) * rc)
        s = logit_fn(s_ref[idx, rows, :], rows)
        if mask_pos is not None:
            qpos = mask_pos[0] + c * rc + lax.broadcasted_iota(jnp.int32, (rc, tk), 0)
            kpos = mask_pos[1] + lax.broadcasted_iota(jnp.int32, (rc, tk), 1)
            s = jnp.where(kpos <= qpos, s, NEG_INF)
        m_prev = m_ref[idx, rows, :]
        m_new = jnp.maximum(m_prev, jnp.max(s, axis=-1, keepdims=True))
        alpha = jnp.exp(m_prev - m_new)
        p = jnp.exp(s - _lane_tile(m_new, tk))
        l_ref[idx, rows, :] = alpha * l_ref[idx, rows, :] + jnp.sum(p, axis=-1, keepdims=True)
        m_ref[idx, rows, :] = m_new
        a_ref[idx, rows, :] = alpha
        p_ref[idx, rows, :] = p.astype(BF16)


def _flash_init(ki, m_ref, l_ref, acc_ref):
    @pl.when(ki == 0)
    def _():
        m_ref[...] = jnp.full(m_ref.shape, NEG_INF, F32)
        l_ref[...] = jnp.zeros(l_ref.shape, F32)
        acc_ref[...] = jnp.zeros(acc_ref.shape, F32)


def _causal_steps(n):
    pairs = [(qi, ki) for qi in range(n) for ki in range(qi + 1)]
    return (jnp.array([p[0] for p in pairs], jnp.int32), jnp.array([p[1] for p in pairs], jnp.int32))


def _flash_scratch(nh, t, width):
    return [pltpu.VMEM((nh, t, t), F32), pltpu.VMEM((nh, t, t), BF16),
            pltpu.VMEM((nh, t, LANES), F32), pltpu.VMEM((nh, t, LANES), F32),
            pltpu.VMEM((nh, t, LANES), F32), pltpu.VMEM((nh, t, width), F32)]


def _mla_flash_kernel(qt_ref, kt_ref, q_ref, k_ref, o_ref, s_ref, p_ref, m_ref, l_ref, a_ref, acc_ref):
    step_id = pl.program_id(2)
    qi = qt_ref[step_id]
    ki = kt_ref[step_id]
    tq = q_ref.shape[1]
    tk = k_ref.shape[1]
    nh = s_ref.shape[0]
    _flash_init(ki, m_ref, l_ref, acc_ref)

    def step(masked):
        k = k_ref[0]
        for g in range(nh):
            s_ref[g] = _dot_nt(q_ref[0, :, g * QCAT:(g + 1) * QCAT], k)
        for g in range(nh):
            _softmax_rows(s_ref, p_ref, m_ref, l_ref, a_ref, g, lambda s, rows: s * MLA_SCALE,
                          (qi * tq, ki * tk) if masked else None)
            acc_ref[g] = _lane_tile(a_ref[g], KV_LORA) * acc_ref[g] + _dot(p_ref[g], k[:, 0:KV_LORA])

    @pl.when(ki < qi)
    def _():
        step(False)

    @pl.when(ki == qi)
    def _():
        step(True)
        for g in range(nh):
            o_ref[0, :, g * KV_LORA:(g + 1) * KV_LORA] = (
                acc_ref[g] / _lane_tile(l_ref[g], KV_LORA)).astype(o_ref.dtype)


def _mla_flash(qcat, kcat):
    b, s, _ = qcat.shape
    t = min(ATT_TILE, s)
    qt, kt = _causal_steps(s // t)
    hg = MLA_HEADS_PER_STEP
    grid_spec = pltpu.PrefetchScalarGridSpec(
        num_scalar_prefetch=2,
        grid=(b, H_A // hg, qt.shape[0]),
        in_specs=[pl.BlockSpec((1, t, hg * QCAT), lambda b, h, i, qt, kt: (b, qt[i], h)),
                  pl.BlockSpec((1, t, QCAT), lambda b, h, i, qt, kt: (b, kt[i], 0))],
        out_specs=pl.BlockSpec((1, t, hg * KV_LORA), lambda b, h, i, qt, kt: (b, qt[i], h)),
        scratch_shapes=_flash_scratch(hg, t, KV_LORA),
    )
    return pl.pallas_call(
        _mla_flash_kernel,
        grid_spec=grid_spec,
        out_shape=jax.ShapeDtypeStruct((b, s, H_A * KV_LORA), BF16),
        compiler_params=pltpu.CompilerParams(dimension_semantics=("parallel", "parallel", "arbitrary")),
        name="mla_prompt_attn",
    )(qt, kt, qcat, kcat)


def _fox_flash_kernel(qt_ref, kt_ref, q_ref, k_ref, v_ref, cq_ref, ck_ref, o_ref,
                      s_ref, p_ref, m_ref, l_ref, a_ref, acc_ref):
    step_id = pl.program_id(2)
    qi = qt_ref[step_id]
    ki = kt_ref[step_id]
    tq = q_ref.shape[1]
    tk = k_ref.shape[1]
    npair = s_ref.shape[0] // 2
    _flash_init(ki, m_ref, l_ref, acc_ref)

    def step(masked):
        lane = lax.broadcasted_iota(jnp.int32, (tq, LANES), 1)
        for pr in range(npair):
            q2 = q_ref[0, :, pr * LANES:(pr + 1) * LANES]
            k2 = k_ref[0, :, pr * LANES:(pr + 1) * LANES]
            for hh in range(2):
                keep = (lane >= HD_F) if hh else (lane < HD_F)
                s_ref[2 * pr + hh] = _dot_nt(jnp.where(keep, q2, jnp.zeros_like(q2)), k2)
        for pr in range(npair):
            v2 = v_ref[0, :, pr * LANES:(pr + 1) * LANES]
            for hh in range(2):
                h = 2 * pr + hh

                def logits(s, rows, h=h, pr=pr, hh=hh):
                    return s + _lane_tile(cq_ref[0, h, rows, :], tk) - ck_ref[0, pr, hh:hh + 1, :]
                _softmax_rows(s_ref, p_ref, m_ref, l_ref, a_ref, h, logits,
                              (qi * tq, ki * tk) if masked else None)
                acc_ref[h] = a_ref[h] * acc_ref[h] + _dot(p_ref[h], v2)

    @pl.when(ki < qi)
    def _():
        step(False)

    @pl.when(ki == qi)
    def _():
        step(True)
        lane = lax.broadcasted_iota(jnp.int32, (tq, LANES), 1)
        for pr in range(npair):
            o0 = acc_ref[2 * pr] / l_ref[2 * pr]
            o1 = acc_ref[2 * pr + 1] / l_ref[2 * pr + 1]
            o_ref[0, :, pr * LANES:(pr + 1) * LANES] = jnp.where(lane < HD_F, o0, o1).astype(o_ref.dtype)


def _fox_flash(fq, fk16, fv16, c_col, c_row):
    b, s, _ = fq.shape
    t = min(ATT_TILE, s)
    qt, kt = _causal_steps(s // t)
    pp = FOX_PAIRS_PER_STEP
    qspec = pl.BlockSpec((1, t, pp * LANES), lambda b, h, i, qt, kt: (b, qt[i], h))
    kspec = pl.BlockSpec((1, t, pp * LANES), lambda b, h, i, qt, kt: (b, kt[i], h))
    grid_spec = pltpu.PrefetchScalarGridSpec(
        num_scalar_prefetch=2,
        grid=(b, H_F // (2 * pp), qt.shape[0]),
        in_specs=[qspec, kspec, kspec,
                  pl.BlockSpec((1, 2 * pp, t, LANES), lambda b, h, i, qt, kt: (b, h, qt[i], 0)),
                  pl.BlockSpec((1, pp, 2, t), lambda b, h, i, qt, kt: (b, h, 0, kt[i]))],
        out_specs=qspec,
        scratch_shapes=_flash_scratch(2 * pp, t, LANES),
    )
    return pl.pallas_call(
        _fox_flash_kernel,
        grid_spec=grid_spec,
        out_shape=jax.ShapeDtypeStruct((b, s, H_F * HD_F), BF16),
        compiler_params=pltpu.CompilerParams(dimension_semantics=("parallel", "parallel", "arbitrary")),
        name="fox_prompt_attn",
    )(qt, kt, fq, fk16, fv16, c_col, c_row)


def _mem_attn_kernel(q_ref, k_ref, v_ref, o_ref):
    s = _dot_nt(q_ref[0], k_ref[0].astype(BF16)) * MEM_SCALE
    m = jnp.max(s, axis=-1, keepdims=True)
    p = jnp.exp(s - m)
    l = jnp.sum(p, axis=-1, keepdims=True)
    o_ref[0] = (_dot(p.astype(BF16), v_ref[0].astype(BF16)) / l).astype(o_ref.dtype)


def _mem_attn(mq, mem_k, mem_v):
    b, s, _ = mq.shape
    nm = mem_k.shape[1]
    t = min(ATT_TILE, s)
    return pl.pallas_call(
        _mem_attn_kernel,
        grid=(b, H_M, s // t),
        in_specs=[pl.BlockSpec((1, t, HD_M), lambda b, h, qi: (b, qi, h)),
                  pl.BlockSpec((1, nm, HD_M), lambda b, h, qi: (b, 0, h)),
                  pl.BlockSpec((1, nm, HD_M), lambda b, h, qi: (b, 0, h))],
        out_specs=pl.BlockSpec((1, t, HD_M), lambda b, h, qi: (b, qi, h)),
        out_shape=jax.ShapeDtypeStruct((b, s, H_M * HD_M), BF16),
        compiler_params=pltpu.CompilerParams(
            dimension_semantics=("parallel", "parallel", "parallel")),
        name="mem_prompt_attn",
    )(mq, mem_k, mem_v)


def _merge_kernel(x_ref, olat_ref, of_ref, om_ref, wg_ref, bg_ref, wuv_ref, wa_ref, wf_ref, wm_ref,
                  wo_ref, g_ref, b_ref, h_ref, oa_ref):
    x = x_ref[...]
    xb = x.astype(BF16)
    for h in range(H_A):
        oa_ref[:, h * V_A:(h + 1) * V_A] = _dot(
            olat_ref[:, h * KV_LORA:(h + 1) * KV_LORA], wuv_ref[h]).astype(BF16)
    branches = (_dot(oa_ref[...], wa_ref[...]), _dot(of_ref[...], wf_ref[...]),
                _dot(om_ref[...], wm_ref[...]))
    merged = None
    for i, br in enumerate(branches):
        gate = _sigmoid(_dot(xb, wg_ref[:, i * D_MODEL:(i + 1) * D_MODEL])
                        + bg_ref[:, i * D_MODEL:(i + 1) * D_MODEL])
        merged = gate * br if merged is None else merged + gate * br
    pre = ALPHA * x + _dot(merged.astype(BF16), wo_ref[...])
    h_ref[...] = _layernorm(pre, g_ref[...], b_ref[...])


def _merge(x2d, o_lat, o_f, o_m, wp):
    t = x2d.shape[0]
    tm = min(ROW_TILE, t)
    row = lambda n: pl.BlockSpec((tm, n), lambda i: (i, 0))
    names = ("w_g", "b_g", "w_uv", "w_br_a", "w_br_f", "w_br_m", "w_out", "ln1_g", "ln1_b")
    return pl.pallas_call(
        _merge_kernel,
        grid=(t // tm,),
        in_specs=[row(D_MODEL), row(H_A * KV_LORA), row(H_F * HD_F), row(H_M * HD_M)]
        + [_const_spec(wp[n].shape) for n in names],
        out_specs=row(D_MODEL),
        out_shape=jax.ShapeDtypeStruct((t, D_MODEL), F32),
        scratch_shapes=[pltpu.VMEM((tm, H_A * V_A), BF16)],
        compiler_params=pltpu.CompilerParams(dimension_semantics=("parallel",)),
        name="merge_ln1",
    )(x2d, o_lat, o_f, o_m, *[wp[n] for n in names])


def _mlp_kernel(h_ref, wg_ref, wu_ref, wd_ref, g_ref, b_ref, y_ref, a_ref, *, n_chunks):
    h = h_ref[...]
    hb = h.astype(BF16)
    fc = wg_ref.shape[1] // n_chunks
    for c in range(n_chunks):
        gt = _dot(hb, wg_ref[:, c * fc:(c + 1) * fc])
        up = _dot(hb, wu_ref[:, c * fc:(c + 1) * fc])
        a_ref[:, c * fc:(c + 1) * fc] = (gt * _sigmoid(gt) * up).astype(BF16)
    y_ref[...] = _layernorm(ALPHA * h + _dot(a_ref[...], wd_ref[...]), g_ref[...], b_ref[...])


def _mlp(h2d, wp):
    t = h2d.shape[0]
    tm = min(ROW_TILE, t)
    d_ff = wp["w_gate"].shape[1]
    n_chunks = 2
    row = lambda n: pl.BlockSpec((tm, n), lambda i: (i, 0))
    names = ("w_gate", "w_up", "w_down", "ln2_g", "ln2_b")
    return pl.pallas_call(
        functools.partial(_mlp_kernel, n_chunks=n_chunks),
        grid=(t // tm,),
        in_specs=[row(D_MODEL)] + [_const_spec(wp[n].shape) for n in names],
        out_specs=row(D_MODEL),
        out_shape=jax.ShapeDtypeStruct((t, D_MODEL), F32),
        scratch_shapes=[pltpu.VMEM((tm, d_ff), BF16)],
        compiler_params=pltpu.CompilerParams(dimension_semantics=("parallel",)),
        name="mlp_ln2",
    )(h2d, *[wp[n] for n in names])


def _pool_suffix_kernel(x_ref, o_ref):
    pb = x_ref.shape[0]
    l = x_ref[...].reshape(pb * H_F, LANES)
    r = lax.broadcasted_iota(jnp.int32, (LANES, LANES), 0)
    c = lax.broadcasted_iota(jnp.int32, (LANES, LANES), 1)
    excl = _exact_dot_ones(l, (r > c).astype(F32))
    tot = jnp.broadcast_to(excl[:, 0:1] + l[:, 0:1], excl.shape)
    o_ref[:, 0:H_F, :] = excl.reshape(pb, H_F, LANES)
    o_ref[:, H_F:2 * H_F, :] = tot.reshape(pb, H_F, LANES)


def _pool_suffix(logf_t):
    n_pool = logf_t.shape[0]
    pb = min(POOL_PAGES_PER_STEP, n_pool)
    return pl.pallas_call(
        _pool_suffix_kernel,
        grid=(n_pool // pb,),
        in_specs=[pl.BlockSpec((pb, H_F, LANES), lambda i: (i, 0, 0))],
        out_specs=pl.BlockSpec((pb, 2 * H_F, LANES), lambda i: (i, 0, 0)),
        out_shape=jax.ShapeDtypeStruct((n_pool, 2 * H_F, LANES), F32),
        compiler_params=pltpu.CompilerParams(dimension_semantics=("parallel",)),
        name="pool_logf_suffix",
    )(logf_t)


def _rows_per_query(x, nq):
    return jnp.concatenate([jnp.broadcast_to(x[q:q + 1], (8, x.shape[1])) for q in range(nq)], axis=0)


def _page_copies(pt_ref, pools, bufs, sem, b, group, slot, pg, npages):
    copies = []
    for i in range(pg):
        pid = pt_ref[b * npages + group * pg + i]
        for a, (pool, buf) in enumerate(zip(pools, bufs)):
            copies.append(pltpu.make_async_copy(pool.at[pid], buf.at[slot, i], sem.at[slot, a]))
    return copies


def _decode_kernel(pt_ref, qcat_ref, fq_ref, latn_ref, krn_ref, fkn_ref, fvn_ref, lfn_ref,
                   lat_hbm, kr_hbm, fk_hbm, fv_hbm, et_hbm, olat_ref, of_ref,
                   lat_buf, kr_buf, fk_buf, fv_buf, et_buf, sem,
                   latc_ref, krc_ref, fkc_ref, fvc_ref, pm_ref, pl_ref, pacca_ref, paccf_ref,
                   *, pg, ng, nq, npages):
    b = pl.program_id(0)
    nb = pl.num_programs(0)
    nrow = nq * 8
    page = lat_buf.shape[2]
    pools = (lat_hbm, kr_hbm, fk_hbm, fv_hbm, et_hbm)
    bufs = (lat_buf, kr_buf, fk_buf, fv_buf, et_buf)
    copies = functools.partial(_page_copies, pt_ref, pools, bufs, sem, pg=pg, npages=npages)

    @pl.when(b == 0)
    def _():
        for c in copies(b, ng - 1, 0):
            c.start()

    row512 = lax.broadcasted_iota(jnp.int32, (nrow, H_F * HD_F), 0)
    lane512 = lax.broadcasted_iota(jnp.int32, (nrow, H_F * HD_F), 1)
    head_mask = (lane512 // HD_F) == (row512 % 8)

    def store_partial(idx, k, m, l, acc, acc_ref):
        pm_ref[idx, k] = jnp.broadcast_to(m, (nrow, LANES))
        pl_ref[idx, k] = jnp.broadcast_to(l, (nrow, LANES))
        acc_ref[k] = acc

    qrow = lax.broadcasted_iota(jnp.int32, (nrow, 1), 0) // 8
    qbd = jnp.where(head_mask, _rows_per_query(fq_ref[0].astype(F32), nq), 0.0)
    qbd16 = qbd.astype(BF16)
    lfn = lfn_ref[0]
    r8 = lax.broadcasted_iota(jnp.int32, (nrow, H_F), 0)
    l8 = lax.broadcasted_iota(jnp.int32, (nrow, H_F), 1)
    pick = l8 == (r8 % 8)
    cum = []
    for q in range(nq):
        cum.append(lfn[q:q + 1] if q == 0 else cum[-1] + lfn[q:q + 1])
    cq = jnp.sum(jnp.where(pick, jnp.concatenate(
        [jnp.broadcast_to(c, (8, H_F)) for c in cum], axis=0), 0.0), axis=1, keepdims=True)
    fkn = fkn_ref[0]
    fvn = fvn_ref[0]
    sf = []
    for s in range(nq):
        ck = jnp.sum(jnp.where(pick, jnp.broadcast_to(cum[s], (nrow, H_F)), 0.0),
                     axis=1, keepdims=True)
        v = jnp.sum(qbd * fkn[s:s + 1], axis=1, keepdims=True) + cq - ck
        sf.append(jnp.where(qrow >= s, v, NEG_INF))
    qc16 = qcat_ref[0]
    qc = qc16.astype(F32)
    latn = latn_ref[0]
    krn = krn_ref[0]
    sa = []
    for s in range(nq):
        v = (jnp.sum(qc[:, 0:KV_LORA] * latn[s:s + 1], axis=1, keepdims=True)
             + jnp.sum(qc[:, KV_LORA:KV_LORA + ROPE] * krn[s:s + 1], axis=1, keepdims=True))
        sa.append(jnp.where(qrow >= s, v * MLA_SCALE, NEG_INF))
    for idx, (sc, vals, acc_ref) in enumerate(((sa, latn, pacca_ref), (sf, fvn, paccf_ref))):
        m = sc[0]
        for s in range(1, nq):
            m = jnp.maximum(m, sc[s])
        l = jnp.zeros_like(m)
        acc = jnp.zeros(acc_ref.shape[1:], F32)
        for s in range(nq):
            p = jnp.exp(sc[s] - m)
            l = l + p
            acc = acc + p * vals[s:s + 1]
        store_partial(idx, ng, m, l, acc, acc_ref)

    carry = jnp.zeros((H_F, LANES), F32)
    for k in range(ng):
        slot = k % 2
        group = ng - 1 - k
        if k + 1 < ng:
            for c in copies(b, group - 1, 1 - slot):
                c.start()
        else:
            @pl.when(b + 1 < nb)
            def _():
                for c in copies(b + 1, ng - 1, 1 - slot):
                    c.start()
        for c in copies(b, group, slot):
            c.wait()

        for i in range(pg):
            sl = slice(i * page, (i + 1) * page)
            latc_ref[slot, sl, :] = lat_buf[slot, i].astype(BF16)
            krc_ref[slot, :, sl] = kr_buf[slot, i].astype(BF16)
            fkc_ref[slot, :, sl] = fk_buf[slot, i].astype(BF16)
            fvc_ref[slot, :, sl] = fv_buf[slot, i].astype(BF16)

        pieces = [None] * pg
        for i in reversed(range(pg)):
            et = et_buf[slot, i]
            pieces[i] = et[0:H_F] + carry
            carry = carry + et[H_F:2 * H_F]
        bias = jnp.concatenate(pieces, axis=1)
        bias = jnp.concatenate([bias] * nq, axis=0)

        sf = _dot(qbd16, fkc_ref[slot]) + bias + cq
        m = jnp.max(sf, axis=-1, keepdims=True)
        p = jnp.exp(sf - m)
        store_partial(1, k, m, jnp.sum(p, axis=-1, keepdims=True),
                      _dot_nt(p.astype(BF16), fvc_ref[slot]), paccf_ref)
        latc = latc_ref[slot]
        sa = (_dot_nt(qc16[:, 0:KV_LORA], latc)
              + _dot(qc16[:, KV_LORA:KV_LORA + ROPE], krc_ref[slot])) * MLA_SCALE
        m = jnp.max(sa, axis=-1, keepdims=True)
        p = jnp.exp(sa - m)
        store_partial(0, k, m, jnp.sum(p, axis=-1, keepdims=True), _dot(p.astype(BF16), latc), pacca_ref)

    outs = []
    for idx, acc_ref in enumerate((pacca_ref, paccf_ref)):
        width = acc_ref.shape[2]
        m = pm_ref[idx, 0]
        for k in range(1, ng + 1):
            m = jnp.maximum(m, pm_ref[idx, k])
        l = jnp.zeros((nrow, LANES), F32)
        acc = jnp.zeros((nrow, width), F32)
        for k in range(ng + 1):
            w = jnp.exp(pm_ref[idx, k] - m)
            l = l + w * pl_ref[idx, k]
            acc = acc + _lane_tile(w, width) * acc_ref[k]
        outs.append(acc / _lane_tile(l, width))
    olat_ref[0] = outs[0].astype(olat_ref.dtype)
    of = jnp.where(head_mask, outs[1], 0.0)
    of_ref[0] = jnp.sum(of.reshape(nq, 8, H_F * HD_F), axis=1).astype(of_ref.dtype)


def _decode_attn(page_table, qcat_s, fq_s, lat_s, krope_s, fk_s, fv_s, logf_s,
                 pool_lat, pool_kr, pool_fk, pool_fv, pool_et):
    nb, npages = page_table.shape
    nq = fq_s.shape[1]
    page = pool_lat.shape[1]
    pg = min(DECODE_PAGES_PER_GROUP, npages)
    ng = npages // pg
    assert ng * pg == npages and ng % 2 == 0, "page groups must alternate between the two buffer slots"
    nrow = nq * 8
    d_f = H_F * HD_F
    pt_flat = page_table.reshape(-1)

    def per_b(shape):
        return pl.BlockSpec((1,) + shape, lambda b, pt: (b,) + (0,) * len(shape))

    hbm = pl.BlockSpec(memory_space=pl.ANY)
    pools = (pool_lat, pool_kr, pool_fk, pool_fv, pool_et)
    grid_spec = pltpu.PrefetchScalarGridSpec(
        num_scalar_prefetch=1,
        grid=(nb,),
        in_specs=[per_b((nrow, QCAT)), per_b((nq, d_f)), per_b((nq, KV_LORA)), per_b((nq, ROPE)),
                  per_b((nq, d_f)), per_b((nq, d_f)), per_b((nq, H_F))] + [hbm] * len(pools),
        out_specs=(per_b((nrow, KV_LORA)), per_b((nq, d_f))),
        scratch_shapes=[pltpu.VMEM((2, pg) + arr.shape[1:], F32) for arr in pools] + [
            pltpu.SemaphoreType.DMA((2, len(pools))),
            pltpu.VMEM((2, pg * page, KV_LORA), BF16), pltpu.VMEM((2, ROPE, pg * page), BF16),
            pltpu.VMEM((2, d_f, pg * page), BF16), pltpu.VMEM((2, d_f, pg * page), BF16),
            pltpu.VMEM((2, ng + 1, nrow, LANES), F32), pltpu.VMEM((2, ng + 1, nrow, LANES), F32),
            pltpu.VMEM((ng + 1, nrow, KV_LORA), F32), pltpu.VMEM((ng + 1, nrow, d_f), F32),
        ],
    )
    return pl.pallas_call(
        functools.partial(_decode_kernel, pg=pg, ng=ng, nq=nq, npages=npages),
        grid_spec=grid_spec,
        out_shape=(jax.ShapeDtypeStruct((nb, nrow, KV_LORA), BF16),
                   jax.ShapeDtypeStruct((nb, nq, d_f), BF16)),
        compiler_params=pltpu.CompilerParams(dimension_semantics=("arbitrary",)),
        name="decode_attn",
    )(pt_flat, qcat_s, fq_s, lat_s, krope_s, fk_s, fv_s, logf_s, *pools)


def _mem_decode_kernel(q_ref, k_ref, v_ref, o_ref):
    nseq, nrow, _ = q_ref.shape
    nkey = k_ref.shape[1]
    row = lax.broadcasted_iota(jnp.int32, (nrow, nkey), 0)
    col = lax.broadcasted_iota(jnp.int32, (nrow, nkey), 1)
    same_head = (col % H_M) == (row % H_M)
    for i in range(nseq):
        s = _dot_nt(q_ref[i], k_ref[i].astype(BF16)) * MEM_SCALE
        s = jnp.where(same_head, s, NEG_INF)
        m = jnp.max(s, axis=-1, keepdims=True)
        p = jnp.exp(s - m)
        l = jnp.sum(p, axis=-1, keepdims=True)
        o_ref[i] = (_dot(p.astype(BF16), v_ref[i].astype(BF16)) / l).astype(o_ref.dtype)


def _mem_decode(mq_s, mem_k, mem_v):
    nb, nrow, d = mq_s.shape
    nkey = mem_k.shape[1]
    nseq = min(MEM_DECODE_SEQS_PER_STEP, nb)
    return pl.pallas_call(
        _mem_decode_kernel,
        grid=(nb // nseq,),
        in_specs=[pl.BlockSpec((nseq, nrow, d), lambda b: (b, 0, 0)),
                  pl.BlockSpec((nseq, nkey, d), lambda b: (b, 0, 0)),
                  pl.BlockSpec((nseq, nkey, d), lambda b: (b, 0, 0))],
        out_specs=pl.BlockSpec((nseq, nrow, d), lambda b: (b, 0, 0)),
        out_shape=jax.ShapeDtypeStruct((nb, nrow, d), BF16),
        compiler_params=pltpu.CompilerParams(dimension_semantics=("parallel",)),
        name="mem_decode_attn",
    )(mq_s, mem_k, mem_v)


def _rope_tables(pos):
    half = ROPE // 2
    inv = jnp.exp(-math.log(ROPE_THETA) * jnp.arange(half, dtype=F32) / half)
    ang = pos.astype(F32)[:, None] * inv
    cos, sin = jnp.cos(ang), jnp.sin(ang)
    pad = jnp.zeros((pos.shape[0], LANES - ROPE), F32)
    return (jnp.concatenate([cos, cos, pad], axis=1), jnp.concatenate([-sin, sin, pad], axis=1))


def _prepare_weights(W_in, b_forget, b_gate, q_norm_g, W_uq, kv_norm_g, W_uk, W_uv,
                     W_br_a, W_br_f, W_br_m, W_out, ln1_g, ln1_b, w_gate, w_up, w_down, ln2_g, ln2_b):
    splits = (Q_LORA, KV_LORA, ROPE, H_F * HD_F, H_F * HD_F, H_F * HD_F, H_F, H_M * HD_M, 3 * D_MODEL)
    offs = [0]
    for n in splits:
        offs.append(offs[-1] + n)
    wq, wkv, wkr, wfq, wfk, wfv, wf, wmq, wg = (W_in[:, offs[i]:offs[i + 1]] for i in range(9))
    pad_to = lambda w: jnp.pad(w, ((0, 0), (0, LANES - w.shape[1])))
    wp = {}
    wp["w_front"] = jnp.concatenate([wq, wkv, wfq, wfk, wfv, wmq, pad_to(wkr), pad_to(wf)], axis=1).astype(BF16)
    w_nope = W_uq[:, :, :NOPE].reshape(Q_LORA, H_A * NOPE)
    w_rope = jnp.pad(W_uq[:, :, NOPE:], ((0, 0), (0, 0), (0, LANES - ROPE))).reshape(Q_LORA, H_A * LANES)
    wp["w_uq"] = jnp.concatenate([w_nope, w_rope], axis=1).astype(BF16)
    wp["w_uk"] = jnp.transpose(W_uk, (1, 2, 0)).astype(BF16)
    wp["w_uv"] = jnp.transpose(W_uv, (1, 0, 2)).astype(BF16)
    wp["q_g"] = q_norm_g.reshape(1, Q_LORA)
    wp["kv_g"] = kv_norm_g.reshape(1, KV_LORA)
    wp["b_f"] = jnp.pad(b_forget, (0, LANES - H_F)).reshape(1, LANES)
    wp["w_g"] = wg.astype(BF16)
    wp["b_g"] = b_gate.reshape(1, 3 * D_MODEL)
    wp["w_br_a"] = W_br_a.astype(BF16)
    wp["w_br_f"] = W_br_f.astype(BF16)
    wp["w_br_m"] = W_br_m.astype(BF16)
    wp["w_out"] = W_out.astype(BF16)
    wp["ln1_g"] = ln1_g.reshape(1, D_MODEL)
    wp["ln1_b"] = ln1_b.reshape(1, D_MODEL)
    wp["w_gate"] = w_gate.astype(BF16)
    wp["w_up"] = w_up.astype(BF16)
    wp["w_down"] = w_down.astype(BF16)
    wp["ln2_g"] = ln2_g.reshape(1, D_MODEL)
    wp["ln2_b"] = ln2_b.reshape(1, D_MODEL)
    return wp


def kernel(x_prompt, x_sample, cache_mla_latent, cache_mla_krope, cache_fox_k, cache_fox_v, cache_fox_logf, cache_mem_k, cache_mem_v, page_table, mem_prompt, W_in, b_forget, b_gate, q_norm_g, W_uq, kv_norm_g, W_uk, W_uv, W_mem_k, W_mem_v, W_br_a, W_br_f, W_br_m, W_out, ln1_g, ln1_b, w_gate, w_up, w_down, ln2_g, ln2_b):
    wp = _prepare_weights(W_in, b_forget, b_gate, q_norm_g, W_uq, kv_norm_g, W_uk, W_uv,
                          W_br_a, W_br_f, W_br_m, W_out, ln1_g, ln1_b, w_gate, w_up, w_down, ln2_g, ln2_b)
    bp, sp, _ = x_prompt.shape
    bs, ss, _ = x_sample.shape
    n_pool, page, _ = cache_mla_latent.shape
    n_mem = mem_prompt.shape[1]
    past_len = page_table.shape[1] * page

    xp = x_prompt.reshape(bp * sp, D_MODEL)
    cos_p, sin_p = _rope_tables(jnp.arange(sp, dtype=jnp.int32))
    (lat_p, krope_p, fk_p, fv_p, logf_p, qcat_p, kcat_p, fq_p, fk16_p, fv16_p, mq_p) = _front(xp, cos_p, sin_p, wp)
    w_mem = jnp.concatenate([W_mem_k.reshape(D_MODEL, H_M * HD_M),
                             W_mem_v.reshape(D_MODEL, H_M * HD_M)], axis=1).astype(BF16)
    mem_kv = _matmul(mem_prompt.reshape(bp * n_mem, D_MODEL), w_mem)
    mem_k_p = mem_kv[:, :H_M * HD_M].reshape(bp, n_mem, H_M * HD_M)
    mem_v_p = mem_kv[:, H_M * HD_M:].reshape(bp, n_mem, H_M * HD_M)

    logf_t = jnp.transpose(logf_p.reshape(bp, sp, H_F), (0, 2, 1))
    c = _cumsum_lanes(logf_t.reshape(bp * H_F, sp))
    c_row = c.reshape(bp, H_F // 2, 2, sp)
    c_col = jnp.broadcast_to(c.reshape(bp, H_F, sp, 1), (bp, H_F, sp, LANES))

    o_lat_p = _mla_flash(qcat_p.reshape(bp, sp, H_A * QCAT), kcat_p.reshape(bp, sp, QCAT))
    o_f_p = _fox_flash(fq_p.reshape(bp, sp, -1), fk16_p.reshape(bp, sp, -1), fv16_p.reshape(bp, sp, -1),
                       c_col, c_row)
    o_m_p = _mem_attn(mq_p.reshape(bp, sp, -1), mem_k_p, mem_v_p)
    h_p = _merge(xp, o_lat_p.reshape(bp * sp, -1), o_f_p.reshape(bp * sp, -1), o_m_p.reshape(bp * sp, -1), wp)
    y_p = _mlp(h_p, wp)

    xs = x_sample.reshape(bs * ss, D_MODEL)
    tm_s = min(ROW_TILE, bs * ss)
    pos_s = past_len + (jnp.arange(tm_s, dtype=jnp.int32) % ss)
    cos_s, sin_s = _rope_tables(pos_s)
    (lat_s, krope_s, fk_s, fv_s, logf_s, qcat_s, _, fq_s, _, _, mq_s) = _front(xs, cos_s, sin_s, wp)
    pool_et = _pool_suffix(jnp.transpose(cache_fox_logf, (0, 2, 1)))
    pool_kr_t = jnp.transpose(cache_mla_krope, (0, 2, 1))
    pool_fk_t = jnp.transpose(cache_fox_k, (0, 2, 3, 1)).reshape(n_pool, H_F * HD_F, page)
    pool_fv_t = jnp.transpose(cache_fox_v, (0, 2, 3, 1)).reshape(n_pool, H_F * HD_F, page)
    o_lat_s, o_f_s = _decode_attn(
        page_table, qcat_s.reshape(bs, ss * H_A, QCAT), fq_s.reshape(bs, ss, -1),
        lat_s.reshape(bs, ss, -1), krope_s.reshape(bs, ss, -1), fk_s.reshape(bs, ss, -1),
        fv_s.reshape(bs, ss, -1), logf_s.reshape(bs, ss, -1),
        cache_mla_latent, pool_kr_t, pool_fk_t, pool_fv_t, pool_et)
    o_m_s = _mem_decode(mq_s.reshape(bs, ss * H_M, HD_M), cache_mem_k.reshape(bs, n_mem * H_M, HD_M),
                        cache_mem_v.reshape(bs, n_mem * H_M, HD_M))
    h_s = _merge(xs, o_lat_s.reshape(bs * ss, -1), o_f_s.reshape(bs * ss, -1), o_m_s.reshape(bs * ss, -1), wp)
    y_s = _mlp(h_s, wp)

    return (y_p.reshape(bp, sp, D_MODEL), y_s.reshape(bs, ss, D_MODEL),
            lat_p.reshape(bp, sp, KV_LORA), krope_p.reshape(bp, sp, ROPE),
            fk_p.reshape(bp, sp, H_F, HD_F), fv_p.reshape(bp, sp, H_F, HD_F), logf_p.reshape(bp, sp, H_F),
            mem_k_p.reshape(bp, n_mem, H_M, HD_M), mem_v_p.reshape(bp, n_mem, H_M, HD_M),
            lat_s.reshape(bs, ss, KV_LORA), krope_s.reshape(bs, ss, ROPE),
            fk_s.reshape(bs, ss, H_F, HD_F), fv_s.reshape(bs, ss, H_F, HD_F), logf_s.reshape(bs, ss, H_F))
```

```python
import functools
import math

import jax
import jax.numpy as jnp
from jax import lax
from jax.experimental import pallas as pl
from jax.experimental.pallas import tpu as pltpu

F32 = jnp.float32
BF16 = jnp.bfloat16

H_A, NOPE, ROPE, V_A = 8, 128, 64, 128
Q_LORA, KV_LORA = 512, 256
H_F, HD_F = 8, 64
H_M, HD_M = 4, 128
D_MODEL = 1024
ROPE_THETA = 10000.0
LN_EPS = 1e-5
RMS_EPS = 1e-6
DEPTH = 1
ALPHA = (2 * DEPTH) ** 0.25
MLA_SCALE = (NOPE + ROPE) ** -0.5
FOX_SCALE = HD_F ** -0.5
MEM_SCALE = HD_M ** -0.5

LANES = 128
QCAT = KV_LORA + LANES
NEG_INF = float("-inf")

ROW_TILE = 256
ATT_TILE = 512
SOFTMAX_ROWS = 32
MLA_HEADS_PER_STEP = 4
FOX_PAIRS_PER_STEP = 2
MEM_DECODE_SEQS_PER_STEP = 4
CUMSUM_CHUNK = 256
DECODE_PAGES_PER_GROUP = 8
POOL_PAGES_PER_STEP = 256


def _dot(a, b):
    return jnp.dot(a, b, preferred_element_type=F32)


def _dot_nt(a, b):
    return lax.dot_general(a, b, (((1,), (1,)), ((), ())), preferred_element_type=F32)


def _const_spec(shape):
    zeros = (0,) * len(shape)
    return pl.BlockSpec(shape, lambda *_: zeros, pipeline_mode=pl.Buffered(1))


def _split3(x):
    hi = x.astype(BF16).astype(F32)
    r = x - hi
    mid = r.astype(BF16).astype(F32)
    lo = (r - mid).astype(BF16).astype(F32)
    return hi, mid, lo


def _exact_dot_ones(x, tri):
    hi, mid, lo = _split3(x)
    return _dot(hi, tri) + _dot(mid, tri) + _dot(lo, tri)


def _layernorm(v, g, b):
    mu = jnp.mean(v, axis=-1, keepdims=True)
    d = v - mu
    var = jnp.mean(d * d, axis=-1, keepdims=True)
    return d * lax.rsqrt(var + LN_EPS) * g + b


def _rmsnorm(v, g):
    return v * lax.rsqrt(jnp.mean(v * v, axis=-1, keepdims=True) + RMS_EPS) * g


def _sigmoid(v):
    return 1.0 / (1.0 + jnp.exp(-v))


def _rope_tile(x, cos, sin_signed):
    lane = lax.broadcasted_iota(jnp.int32, x.shape, 1)
    rot = jnp.where(lane < ROPE // 2, pltpu.roll(x, LANES - ROPE // 2, 1), pltpu.roll(x, ROPE // 2, 1))
    return x * cos + rot * sin_signed


def _front_kernel(x_ref, cos_ref, sin_ref, w_ref, wuq_ref, wuk_ref, qg_ref, kvg_ref, bf_ref,
                  lat_ref, krope_ref, fk_ref, fv_ref, logf_ref,
                  qcat_ref, kcat_ref, fq_ref, fk16_ref, fv16_ref, mq_ref):
    xb = x_ref[...].astype(BF16)
    z = _dot(xb, w_ref[...])
    cos = cos_ref[...]
    sin = sin_ref[...]
    qn = _rmsnorm(z[:, 0:512], qg_ref[...]).astype(BF16)
    q = _dot(qn, wuq_ref[...])
    for h in range(H_A):
        q_nope = q[:, h * NOPE:(h + 1) * NOPE].astype(BF16)
        qcat_ref[:, h * QCAT:h * QCAT + KV_LORA] = _dot(q_nope, wuk_ref[h]).astype(BF16)
        xr = q[:, H_A * NOPE + h * LANES:H_A * NOPE + (h + 1) * LANES]
        qcat_ref[:, h * QCAT + KV_LORA:(h + 1) * QCAT] = _rope_tile(xr, cos, sin).astype(BF16)
    lat = _rmsnorm(z[:, 512:768], kvg_ref[...])
    lat_ref[...] = lat
    kcat_ref[:, 0:KV_LORA] = lat.astype(BF16)
    kr = _rope_tile(z[:, 2816:2944], cos, sin)
    krope_ref[...] = kr[:, 0:ROPE]
    kcat_ref[:, KV_LORA:QCAT] = kr.astype(BF16)
    fq_ref[...] = (z[:, 768:1280] * FOX_SCALE).astype(BF16)
    zfk = z[:, 1280:1792]
    zfv = z[:, 1792:2304]
    fk_ref[...] = zfk
    fv_ref[...] = zfv
    fk16_ref[...] = zfk.astype(BF16)
    fv16_ref[...] = zfv.astype(BF16)
    mq_ref[...] = z[:, 2304:2816].astype(BF16)
    zf = z[:, 2944:3072] + bf_ref[...]
    logf = jnp.minimum(zf, 0.0) - jnp.log1p(jnp.exp(-jnp.abs(zf)))
    logf_ref[...] = logf[:, 0:H_F]


def _front(x2d, cos_t, sin_t, wp):
    t = x2d.shape[0]
    tm = min(ROW_TILE, t)
    nblk = t // tm
    ntab = cos_t.shape[0] // tm
    row = lambda n: pl.BlockSpec((tm, n), lambda i: (i, 0))
    tab = pl.BlockSpec((tm, LANES), lambda i: (i % ntab, 0))
    out_shape = (
        jax.ShapeDtypeStruct((t, KV_LORA), F32), jax.ShapeDtypeStruct((t, ROPE), F32),
        jax.ShapeDtypeStruct((t, H_F * HD_F), F32), jax.ShapeDtypeStruct((t, H_F * HD_F), F32),
        jax.ShapeDtypeStruct((t, H_F), F32),
        jax.ShapeDtypeStruct((t, H_A * QCAT), BF16), jax.ShapeDtypeStruct((t, QCAT), BF16),
        jax.ShapeDtypeStruct((t, H_F * HD_F), BF16), jax.ShapeDtypeStruct((t, H_F * HD_F), BF16),
        jax.ShapeDtypeStruct((t, H_F * HD_F), BF16), jax.ShapeDtypeStruct((t, H_M * HD_M), BF16),
    )
    out_specs = (row(KV_LORA), row(ROPE), row(512), row(512), row(H_F),
                 row(H_A * QCAT), row(QCAT), row(512), row(512), row(512), row(512))
    return pl.pallas_call(
        _front_kernel,
        grid=(nblk,),
        in_specs=[row(D_MODEL), tab, tab,
                  _const_spec(wp["w_front"].shape), _const_spec(wp["w_uq"].shape),
                  _const_spec(wp["w_uk"].shape), _const_spec((1, Q_LORA)),
                  _const_spec((1, KV_LORA)), _const_spec((1, LANES))],
        out_specs=out_specs,
        out_shape=out_shape,
        compiler_params=pltpu.CompilerParams(dimension_semantics=("parallel",)),
        name="front",
    )(x2d, cos_t, sin_t, wp["w_front"], wp["w_uq"], wp["w_uk"], wp["q_g"], wp["kv_g"], wp["b_f"])


def _mm_kernel(x_ref, w_ref, o_ref):
    o_ref[...] = _dot(x_ref[...].astype(BF16), w_ref[...])


def _matmul(x2d, w):
    t, k = x2d.shape
    n = w.shape[1]
    tm = min(ROW_TILE, t)
    return pl.pallas_call(
        _mm_kernel,
        grid=(t // tm,),
        in_specs=[pl.BlockSpec((tm, k), lambda i: (i, 0)), _const_spec((k, n))],
        out_specs=pl.BlockSpec((tm, n), lambda i: (i, 0)),
        out_shape=jax.ShapeDtypeStruct((t, n), F32),
        compiler_params=pltpu.CompilerParams(dimension_semantics=("parallel",)),
        name="mem_kv_proj",
    )(x2d, w)


def _cumsum_kernel(x_ref, o_ref):
    n = x_ref.shape[1]
    ck = min(CUMSUM_CHUNK, n)
    r = lax.broadcasted_iota(jnp.int32, (ck, ck), 0)
    c = lax.broadcasted_iota(jnp.int32, (ck, ck), 1)
    tri = (r <= c).astype(F32)
    carry = jnp.zeros((x_ref.shape[0], 1), F32)
    for j in range(n // ck):
        cs = _exact_dot_ones(x_ref[:, j * ck:(j + 1) * ck], tri) + carry
        o_ref[:, j * ck:(j + 1) * ck] = cs
        carry = cs[:, ck - 1:ck]


def _cumsum_lanes(x2d):
    return pl.pallas_call(
        _cumsum_kernel,
        out_shape=jax.ShapeDtypeStruct(x2d.shape, F32),
        name="logf_cumsum",
    )(x2d)


def _lane_tile(x, width):
    n = width // LANES
    return x if n == 1 else jnp.concatenate([x] * n, axis=1)


def _softmax_rows(s_ref, p_ref, m_ref, l_ref, a_ref, idx, logit_fn, diagonal):
    tq, tk = s_ref.shape[1:]
    assert not diagonal or tq == tk
    rc = min(SOFTMAX_ROWS, tq)
    for c in range(tq // rc):
        rows = slice(c * rc, (c + 1) * rc)
        w = min(tk, pl.cdiv((c + 1) * rc, LANES) * LANES) if diagonal else tk
        s = logit_fn(s_ref[idx, rows, 0:w], rows, w)
        if diagonal:
            qpos = c * rc + lax.broadcasted_iota(jnp.int32, (rc, w), 0)
            kpos = lax.broadcasted_iota(jnp.int32, (rc, w), 1)
            s = jnp.where(kpos <= qpos, s, NEG_INF)
        m_prev = m_ref[idx, rows, :]
        m_new = jnp.maximum(m_prev, jnp.max(s, axis=-1, keepdims=True))
        alpha = jnp.exp(m_prev - m_new)
        p = jnp.exp(s - _lane_tile(m_new, w))
        l_ref[idx, rows, :] = alpha * l_ref[idx, rows, :] + jnp.sum(p, axis=-1, keepdims=True)
        m_ref[idx, rows, :] = m_new
        a_ref[idx, rows, :] = alpha
        p_ref[idx, rows, 0:w] = p.astype(BF16)
        if w < tk:
            p_ref[idx, rows, w:tk] = jnp.zeros((rc, tk - w), BF16)


def _flash_init(ki, m_ref, l_ref, acc_ref):
    @pl.when(ki == 0)
    def _():
        m_ref[...] = jnp.full(m_ref.shape, NEG_INF, F32)
        l_ref[...] = jnp.zeros(l_ref.shape, F32)
        acc_ref[...] = jnp.zeros(acc_ref.shape, F32)


def _causal_steps(n):
    pairs = [(qi, ki) for qi in range(n) for ki in range(qi + 1)]
    return (jnp.array([p[0] for p in pairs], jnp.int32), jnp.array([p[1] for p in pairs], jnp.int32))


def _flash_scratch(nh, t, width):
    return [pltpu.VMEM((nh, t, t), F32), pltpu.VMEM((nh, t, t), BF16),
            pltpu.VMEM((nh, t, LANES), F32), pltpu.VMEM((nh, t, LANES), F32),
            pltpu.VMEM((nh, t, LANES), F32), pltpu.VMEM((nh, t, width), F32)]


def _mla_flash_kernel(qt_ref, kt_ref, q_ref, k_ref, o_ref, s_ref, p_ref, m_ref, l_ref, a_ref, acc_ref):
    step_id = pl.program_id(2)
    qi = qt_ref[step_id]
    ki = kt_ref[step_id]
    tq = q_ref.shape[1]
    tk = k_ref.shape[1]
    nh = s_ref.shape[0]
    _flash_init(ki, m_ref, l_ref, acc_ref)

    def step(masked):
        k = k_ref[0]
        for g in range(nh):
            s_ref[g] = _dot_nt(q_ref[0, :, g * QCAT:(g + 1) * QCAT], k)
        for g in range(nh):
            _softmax_rows(s_ref, p_ref, m_ref, l_ref, a_ref, g, lambda s, rows, w: s * MLA_SCALE, masked)
            acc_ref[g] = _lane_tile(a_ref[g], KV_LORA) * acc_ref[g] + _dot(p_ref[g], k[:, 0:KV_LORA])

    @pl.when(ki < qi)
    def _():
        step(False)

    @pl.when(ki == qi)
    def _():
        step(True)
        for g in range(nh):
            o_ref[0, :, g * KV_LORA:(g + 1) * KV_LORA] = (
                acc_ref[g] / _lane_tile(l_ref[g], KV_LORA)).astype(o_ref.dtype)


def _mla_flash(qcat, kcat):
    b, s, _ = qcat.shape
    t = min(ATT_TILE, s)
    qt, kt = _causal_steps(s // t)
    hg = MLA_HEADS_PER_STEP
    grid_spec = pltpu.PrefetchScalarGridSpec(
        num_scalar_prefetch=2,
        grid=(b, H_A // hg, qt.shape[0]),
        in_specs=[pl.BlockSpec((1, t, hg * QCAT), lambda b, h, i, qt, kt: (b, qt[i], h)),
                  pl.BlockSpec((1, t, QCAT), lambda b, h, i, qt, kt: (b, kt[i], 0))],
        out_specs=pl.BlockSpec((1, t, hg * KV_LORA), lambda b, h, i, qt, kt: (b, qt[i], h)),
        scratch_shapes=_flash_scratch(hg, t, KV_LORA),
    )
    return pl.pallas_call(
        _mla_flash_kernel,
        grid_spec=grid_spec,
        out_shape=jax.ShapeDtypeStruct((b, s, H_A * KV_LORA), BF16),
        compiler_params=pltpu.CompilerParams(dimension_semantics=("parallel", "parallel", "arbitrary")),
        name="mla_prompt_attn",
    )(qt, kt, qcat, kcat)


def _fox_flash_kernel(qt_ref, kt_ref, q_ref, k_ref, v_ref, cq_ref, ck_ref, o_ref,
                      s_ref, p_ref, m_ref, l_ref, a_ref, acc_ref):
    step_id = pl.program_id(2)
    qi = qt_ref[step_id]
    ki = kt_ref[step_id]
    tq = q_ref.shape[1]
    tk = k_ref.shape[1]
    npair = s_ref.shape[0] // 2
    _flash_init(ki, m_ref, l_ref, acc_ref)

    def step(masked):
        lane = lax.broadcasted_iota(jnp.int32, (tq, LANES), 1)
        for pr in range(npair):
            q2 = q_ref[0, :, pr * LANES:(pr + 1) * LANES]
            k2 = k_ref[0, :, pr * LANES:(pr + 1) * LANES]
            for hh in range(2):
                keep = (lane >= HD_F) if hh else (lane < HD_F)
                s_ref[2 * pr + hh] = _dot_nt(jnp.where(keep, q2, jnp.zeros_like(q2)), k2)
        for pr in range(npair):
            v2 = v_ref[0, :, pr * LANES:(pr + 1) * LANES]
            for hh in range(2):
                h = 2 * pr + hh

                def logits(s, rows, w, h=h, pr=pr, hh=hh):
                    return s + _lane_tile(cq_ref[0, h, rows, :], w) - ck_ref[0, pr, hh:hh + 1, 0:w]
                _softmax_rows(s_ref, p_ref, m_ref, l_ref, a_ref, h, logits, masked)
                acc_ref[h] = a_ref[h] * acc_ref[h] + _dot(p_ref[h], v2)

    @pl.when(ki < qi)
    def _():
        step(False)

    @pl.when(ki == qi)
    def _():
        step(True)
        lane = lax.broadcasted_iota(jnp.int32, (tq, LANES), 1)
        for pr in range(npair):
            o0 = acc_ref[2 * pr] / l_ref[2 * pr]
            o1 = acc_ref[2 * pr + 1] / l_ref[2 * pr + 1]
            o_ref[0, :, pr * LANES:(pr + 1) * LANES] = jnp.where(lane < HD_F, o0, o1).astype(o_ref.dtype)


def _fox_flash(fq, fk16, fv16, c_col, c_row):
    b, s, _ = fq.shape
    t = min(ATT_TILE, s)
    qt, kt = _causal_steps(s // t)
    pp = FOX_PAIRS_PER_STEP
    qspec = pl.BlockSpec((1, t, pp * LANES), lambda b, h, i, qt, kt: (b, qt[i], h))
    kspec = pl.BlockSpec((1, t, pp * LANES), lambda b, h, i, qt, kt: (b, kt[i], h))
    grid_spec = pltpu.PrefetchScalarGridSpec(
        num_scalar_prefetch=2,
        grid=(b, H_F // (2 * pp), qt.shape[0]),
        in_specs=[qspec, kspec, kspec,
                  pl.BlockSpec((1, 2 * pp, t, LANES), lambda b, h, i, qt, kt: (b, h, qt[i], 0)),
                  pl.BlockSpec((1, pp, 2, t), lambda b, h, i, qt, kt: (b, h, 0, kt[i]))],
        out_specs=qspec,
        scratch_shapes=_flash_scratch(2 * pp, t, LANES),
    )
    return pl.pallas_call(
        _fox_flash_kernel,
        grid_spec=grid_spec,
        out_shape=jax.ShapeDtypeStruct((b, s, H_F * HD_F), BF16),
        compiler_params=pltpu.CompilerParams(dimension_semantics=("parallel", "parallel", "arbitrary")),
        name="fox_prompt_attn",
    )(qt, kt, fq, fk16, fv16, c_col, c_row)


def _mem_attn_kernel(q_ref, k_ref, v_ref, o_ref, s_ref, p_ref, l_ref):
    tq = q_ref.shape[1]
    rc = min(SOFTMAX_ROWS, tq)
    heads = [slice(h * HD_M, (h + 1) * HD_M) for h in range(H_M)]
    for h, hs in enumerate(heads):
        s_ref[h] = _dot_nt(q_ref[0, :, hs], k_ref[0, :, hs].astype(BF16))
    for h, hs in enumerate(heads):
        for c in range(tq // rc):
            rows = slice(c * rc, (c + 1) * rc)
            s = s_ref[h, rows, :] * MEM_SCALE
            p = jnp.exp(s - jnp.max(s, axis=-1, keepdims=True))
            l_ref[h, rows, :] = jnp.broadcast_to(jnp.sum(p, axis=-1, keepdims=True), (rc, HD_M))
            p_ref[h, rows, :] = p.astype(BF16)
        o_ref[0, :, hs] = (_dot(p_ref[h], v_ref[0, :, hs].astype(BF16)) / l_ref[h]).astype(o_ref.dtype)


def _mem_attn(mq, mem_k, mem_v):
    b, s, d = mq.shape
    nm = mem_k.shape[1]
    t = min(ATT_TILE, s)
    return pl.pallas_call(
        _mem_attn_kernel,
        grid=(b, s // t),
        in_specs=[pl.BlockSpec((1, t, d), lambda b, qi: (b, qi, 0)),
                  pl.BlockSpec((1, nm, d), lambda b, qi: (b, 0, 0)),
                  pl.BlockSpec((1, nm, d), lambda b, qi: (b, 0, 0))],
        out_specs=pl.BlockSpec((1, t, d), lambda b, qi: (b, qi, 0)),
        out_shape=jax.ShapeDtypeStruct((b, s, d), BF16),
        scratch_shapes=[pltpu.VMEM((H_M, t, nm), F32), pltpu.VMEM((H_M, t, nm), BF16),
                        pltpu.VMEM((H_M, t, HD_M), F32)],
        compiler_params=pltpu.CompilerParams(dimension_semantics=("parallel", "parallel")),
        name="mem_prompt_attn",
    )(mq, mem_k, mem_v)


def _merge_kernel(x_ref, olat_ref, of_ref, om_ref, wg_ref, bg_ref, wuv_ref, wa_ref, wf_ref, wm_ref,
                  wo_ref, g_ref, b_ref, h_ref, oa_ref):
    x = x_ref[...]
    xb = x.astype(BF16)
    for h in range(H_A):
        oa_ref[:, h * V_A:(h + 1) * V_A] = _dot(
            olat_ref[:, h * KV_LORA:(h + 1) * KV_LORA], wuv_ref[h]).astype(BF16)
    branches = (_dot(oa_ref[...], wa_ref[...]), _dot(of_ref[...], wf_ref[...]),
                _dot(om_ref[...], wm_ref[...]))
    merged = None
    for i, br in enumerate(branches):
        gate = _sigmoid(_dot(xb, wg_ref[:, i * D_MODEL:(i + 1) * D_MODEL])
                        + bg_ref[:, i * D_MODEL:(i + 1) * D_MODEL])
        merged = gate * br if merged is None else merged + gate * br
    pre = ALPHA * x + _dot(merged.astype(BF16), wo_ref[...])
    h_ref[...] = _layernorm(pre, g_ref[...], b_ref[...])


def _merge(x2d, o_lat, o_f, o_m, wp):
    t = x2d.shape[0]
    tm = min(ROW_TILE, t)
    row = lambda n: pl.BlockSpec((tm, n), lambda i: (i, 0))
    names = ("w_g", "b_g", "w_uv", "w_br_a", "w_br_f", "w_br_m", "w_out", "ln1_g", "ln1_b")
    return pl.pallas_call(
        _merge_kernel,
        grid=(t // tm,),
        in_specs=[row(D_MODEL), row(H_A * KV_LORA), row(H_F * HD_F), row(H_M * HD_M)]
        + [_const_spec(wp[n].shape) for n in names],
        out_specs=row(D_MODEL),
        out_shape=jax.ShapeDtypeStruct((t, D_MODEL), F32),
        scratch_shapes=[pltpu.VMEM((tm, H_A * V_A), BF16)],
        compiler_params=pltpu.CompilerParams(dimension_semantics=("parallel",)),
        name="merge_ln1",
    )(x2d, o_lat, o_f, o_m, *[wp[n] for n in names])


def _mlp_kernel(h_ref, wg_ref, wu_ref, wd_ref, g_ref, b_ref, y_ref, a_ref, *, n_chunks):
    h = h_ref[...]
    hb = h.astype(BF16)
    fc = wg_ref.shape[1] // n_chunks
    for c in range(n_chunks):
        gt = _dot(hb, wg_ref[:, c * fc:(c + 1) * fc])
        up = _dot(hb, wu_ref[:, c * fc:(c + 1) * fc])
        a_ref[:, c * fc:(c + 1) * fc] = (gt * _sigmoid(gt) * up).astype(BF16)
    y_ref[...] = _layernorm(ALPHA * h + _dot(a_ref[...], wd_ref[...]), g_ref[...], b_ref[...])


def _mlp(h2d, wp):
    t = h2d.shape[0]
    tm = min(ROW_TILE, t)
    d_ff = wp["w_gate"].shape[1]
    n_chunks = 2
    row = lambda n: pl.BlockSpec((tm, n), lambda i: (i, 0))
    names = ("w_gate", "w_up", "w_down", "ln2_g", "ln2_b")
    return pl.pallas_call(
        functools.partial(_mlp_kernel, n_chunks=n_chunks),
        grid=(t // tm,),
        in_specs=[row(D_MODEL)] + [_const_spec(wp[n].shape) for n in names],
        out_specs=row(D_MODEL),
        out_shape=jax.ShapeDtypeStruct((t, D_MODEL), F32),
        scratch_shapes=[pltpu.VMEM((tm, d_ff), BF16)],
        compiler_params=pltpu.CompilerParams(dimension_semantics=("parallel",)),
        name="mlp_ln2",
    )(h2d, *[wp[n] for n in names])


def _rows_per_query(x, nq):
    return jnp.concatenate([jnp.broadcast_to(x[q:q + 1], (8, x.shape[1])) for q in range(nq)], axis=0)


def _page_copies(pt_ref, pools, bufs, sem, b, group, slot, pg, npages):
    copies = []
    for i in range(pg):
        pid = pt_ref[b * npages + group * pg + i]
        for a, (pool, buf) in enumerate(zip(pools, bufs)):
            copy = pltpu.make_async_copy(pool.at[pid], buf.at[slot, i], sem.at[slot, a])
            copies.append((copy, i % 2))
    return copies


def _pool_suffix_kernel(x_ref, o_ref):
    pb = x_ref.shape[0]
    l = x_ref[...].reshape(pb * H_F, LANES)
    r = lax.broadcasted_iota(jnp.int32, (LANES, LANES), 0)
    c = lax.broadcasted_iota(jnp.int32, (LANES, LANES), 1)
    excl = _exact_dot_ones(l, (r > c).astype(F32))
    tot = jnp.broadcast_to(excl[:, 0:1] + l[:, 0:1], excl.shape)
    o_ref[:, 0:H_F, :] = excl.reshape(pb, H_F, LANES)
    o_ref[:, H_F:2 * H_F, :] = tot.reshape(pb, H_F, LANES)


def _pool_suffix(logf_t):
    n_pool = logf_t.shape[0]
    pb = min(POOL_PAGES_PER_STEP, n_pool)
    return pl.pallas_call(
        _pool_suffix_kernel,
        grid=(n_pool // pb,),
        in_specs=[pl.BlockSpec((pb, H_F, LANES), lambda i: (i, 0, 0))],
        out_specs=pl.BlockSpec((pb, 2 * H_F, LANES), lambda i: (i, 0, 0)),
        out_shape=jax.ShapeDtypeStruct((n_pool, 2 * H_F, LANES), F32),
        compiler_params=pltpu.CompilerParams(dimension_semantics=("parallel",)),
        name="pool_logf_suffix",
    )(logf_t)


def _decode_kernel(pt_ref, qcat_ref, fq_ref, latn_ref, krn_ref, fkn_ref, fvn_ref, lfn_ref,
                   lat_hbm, kr_hbm, fk_hbm, fv_hbm, lf_hbm, olat_ref, of_ref,
                   lat_buf, kr_buf, fk_buf, fv_buf, lf_buf, sem,
                   latc_ref, krc_ref, fkc_ref, fvc_ref, pm_ref, pl_ref, pacca_ref, paccf_ref,
                   *, pg, ng, nq, npages):
    b = pl.program_id(0)
    nb = pl.num_programs(0)
    nrow = nq * 8
    page = lat_buf.shape[2]
    pools = (lat_hbm, kr_hbm, fk_hbm, fv_hbm, lf_hbm)
    bufs = (lat_buf, kr_buf, fk_buf, fv_buf, lf_buf)
    copies = functools.partial(_page_copies, pt_ref, pools, bufs, sem, pg=pg, npages=npages)

    @pl.when(b == 0)
    def _():
        for c, prio in copies(b, ng - 1, 0):
            c.start(priority=prio)

    row512 = lax.broadcasted_iota(jnp.int32, (nrow, H_F * HD_F), 0)
    lane512 = lax.broadcasted_iota(jnp.int32, (nrow, H_F * HD_F), 1)
    head_mask = (lane512 // HD_F) == (row512 % 8)

    def store_partial(idx, k, m, l, acc, acc_ref):
        pm_ref[idx, k] = jnp.broadcast_to(m, (nrow, LANES))
        pl_ref[idx, k] = jnp.broadcast_to(l, (nrow, LANES))
        acc_ref[k] = acc

    qrow = lax.broadcasted_iota(jnp.int32, (nrow, 1), 0) // 8
    qbd = jnp.where(head_mask, _rows_per_query(fq_ref[0].astype(F32), nq), 0.0)
    qbd16 = qbd.astype(BF16)
    lfn = lfn_ref[0]
    r8 = lax.broadcasted_iota(jnp.int32, (nrow, H_F), 0)
    l8 = lax.broadcasted_iota(jnp.int32, (nrow, H_F), 1)
    pick = l8 == (r8 % 8)
    cum = []
    for q in range(nq):
        cum.append(lfn[q:q + 1] if q == 0 else cum[-1] + lfn[q:q + 1])
    cq = jnp.sum(jnp.where(pick, jnp.concatenate(
        [jnp.broadcast_to(c, (8, H_F)) for c in cum], axis=0), 0.0), axis=1, keepdims=True)
    fkn = fkn_ref[0]
    fvn = fvn_ref[0]
    sf = []
    for s in range(nq):
        ck = jnp.sum(jnp.where(pick, jnp.broadcast_to(cum[s], (nrow, H_F)), 0.0),
                     axis=1, keepdims=True)
        v = jnp.sum(qbd * fkn[s:s + 1], axis=1, keepdims=True) + cq - ck
        sf.append(jnp.where(qrow >= s, v, NEG_INF))
    qc16 = qcat_ref[0]
    qc = qc16.astype(F32)
    latn = latn_ref[0]
    krn = krn_ref[0]
    sa = []
    for s in range(nq):
        v = (jnp.sum(qc[:, 0:KV_LORA] * latn[s:s + 1], axis=1, keepdims=True)
             + jnp.sum(qc[:, KV_LORA:KV_LORA + ROPE] * krn[s:s + 1], axis=1, keepdims=True))
        sa.append(jnp.where(qrow >= s, v * MLA_SCALE, NEG_INF))
    for idx, (sc, vals, acc_ref) in enumerate(((sa, latn, pacca_ref), (sf, fvn, paccf_ref))):
        m = sc[0]
        for s in range(1, nq):
            m = jnp.maximum(m, sc[s])
        l = jnp.zeros_like(m)
        acc = jnp.zeros(acc_ref.shape[1:], F32)
        for s in range(nq):
            p = jnp.exp(sc[s] - m)
            l = l + p
            acc = acc + p * vals[s:s + 1]
        store_partial(idx, ng, m, l, acc, acc_ref)

    carry = jnp.zeros((H_F, LANES), F32)
    for k in range(ng):
        slot = k % 2
        group = ng - 1 - k
        if k + 1 < ng:
            for c, prio in copies(b, group - 1, 1 - slot):
                c.start(priority=prio)
        else:
            @pl.when(b + 1 < nb)
            def _():
                for c, prio in copies(b + 1, ng - 1, 1 - slot):
                    c.start(priority=prio)
        for c, _ in copies(b, group, slot):
            c.wait()

        for i in range(pg):
            sl = slice(i * page, (i + 1) * page)
            latc_ref[slot, sl, :] = lat_buf[slot, i].astype(BF16)
            krc_ref[slot, :, sl] = kr_buf[slot, i].astype(BF16)
            fkc_ref[slot, :, sl] = fk_buf[slot, i].astype(BF16)
            fvc_ref[slot, :, sl] = fv_buf[slot, i].astype(BF16)

        pieces = [None] * pg
        for i in reversed(range(pg)):
            sums = lf_buf[slot, i]
            pieces[i] = sums[0:H_F] + carry
            carry = carry + sums[H_F:2 * H_F]
        bias = jnp.concatenate(pieces, axis=1)
        bias = jnp.concatenate([bias] * nq, axis=0)

        sf = _dot(qbd16, fkc_ref[slot]) + bias + cq
        m = jnp.max(sf, axis=-1, keepdims=True)
        p = jnp.exp(sf - m)
        store_partial(1, k, m, jnp.sum(p, axis=-1, keepdims=True),
                      _dot_nt(p.astype(BF16), fvc_ref[slot]), paccf_ref)
        latc = latc_ref[slot]
        sa = (_dot_nt(qc16[:, 0:KV_LORA], latc)
              + _dot(qc16[:, KV_LORA:KV_LORA + ROPE], krc_ref[slot])) * MLA_SCALE
        m = jnp.max(sa, axis=-1, keepdims=True)
        p = jnp.exp(sa - m)
        store_partial(0, k, m, jnp.sum(p, axis=-1, keepdims=True), _dot(p.astype(BF16), latc), pacca_ref)

    outs = []
    for idx, acc_ref in enumerate((pacca_ref, paccf_ref)):
        width = acc_ref.shape[2]
        m = pm_ref[idx, 0]
        for k in range(1, ng + 1):
            m = jnp.maximum(m, pm_ref[idx, k])
        l = jnp.zeros((nrow, LANES), F32)
        acc = jnp.zeros((nrow, width), F32)
        for k in range(ng + 1):
            w = jnp.exp(pm_ref[idx, k] - m)
            l = l + w * pl_ref[idx, k]
            acc = acc + _lane_tile(w, width) * acc_ref[k]
        outs.append(acc / _lane_tile(l, width))
    olat_ref[0] = outs[0].astype(olat_ref.dtype)
    of = jnp.where(head_mask, outs[1], 0.0)
    of_ref[0] = jnp.sum(of.reshape(nq, 8, H_F * HD_F), axis=1).astype(of_ref.dtype)


def _decode_attn(page_table, qcat_s, fq_s, lat_s, krope_s, fk_s, fv_s, logf_s,
                 pool_lat, pool_kr, pool_fk, pool_fv, pool_lf):
    nb, npages = page_table.shape
    nq = fq_s.shape[1]
    page = pool_lat.shape[1]
    pg = min(DECODE_PAGES_PER_GROUP, npages)
    ng = npages // pg
    assert ng * pg == npages and ng % 2 == 0, "page groups must alternate between the two buffer slots"
    nrow = nq * 8
    d_f = H_F * HD_F
    pt_flat = page_table.reshape(-1)

    def per_b(shape):
        return pl.BlockSpec((1,) + shape, lambda b, pt: (b,) + (0,) * len(shape))

    hbm = pl.BlockSpec(memory_space=pl.ANY)
    pools = (pool_lat, pool_kr, pool_fk, pool_fv, pool_lf)
    grid_spec = pltpu.PrefetchScalarGridSpec(
        num_scalar_prefetch=1,
        grid=(nb,),
        in_specs=[per_b((nrow, QCAT)), per_b((nq, d_f)), per_b((nq, KV_LORA)), per_b((nq, ROPE)),
                  per_b((nq, d_f)), per_b((nq, d_f)), per_b((nq, H_F))] + [hbm] * len(pools),
        out_specs=(per_b((nrow, KV_LORA)), per_b((nq, d_f))),
        scratch_shapes=[pltpu.VMEM((2, pg) + arr.shape[1:], F32) for arr in pools] + [
            pltpu.SemaphoreType.DMA((2, len(pools))),
            pltpu.VMEM((2, pg * page, KV_LORA), BF16), pltpu.VMEM((2, ROPE, pg * page), BF16),
            pltpu.VMEM((2, d_f, pg * page), BF16), pltpu.VMEM((2, d_f, pg * page), BF16),
            pltpu.VMEM((2, ng + 1, nrow, LANES), F32), pltpu.VMEM((2, ng + 1, nrow, LANES), F32),
            pltpu.VMEM((ng + 1, nrow, KV_LORA), F32), pltpu.VMEM((ng + 1, nrow, d_f), F32),
        ],
    )
    return pl.pallas_call(
        functools.partial(_decode_kernel, pg=pg, ng=ng, nq=nq, npages=npages),
        grid_spec=grid_spec,
        out_shape=(jax.ShapeDtypeStruct((nb, nrow, KV_LORA), BF16),
                   jax.ShapeDtypeStruct((nb, nq, d_f), BF16)),
        compiler_params=pltpu.CompilerParams(dimension_semantics=("arbitrary",)),
        name="decode_attn",
    )(pt_flat, qcat_s, fq_s, lat_s, krope_s, fk_s, fv_s, logf_s, *pools)


def _mem_decode_kernel(q_ref, k_ref, v_ref, o_ref):
    nseq, nrow, _ = q_ref.shape
    nkey = k_ref.shape[1]
    row = lax.broadcasted_iota(jnp.int32, (nrow, nkey), 0)
    col = lax.broadcasted_iota(jnp.int32, (nrow, nkey), 1)
    same_head = (col % H_M) == (row % H_M)
    for i in range(nseq):
        s = _dot_nt(q_ref[i], k_ref[i].astype(BF16)) * MEM_SCALE
        s = jnp.where(same_head, s, NEG_INF)
        m = jnp.max(s, axis=-1, keepdims=True)
        p = jnp.exp(s - m)
        l = jnp.sum(p, axis=-1, keepdims=True)
        o_ref[i] = (_dot(p.astype(BF16), v_ref[i].astype(BF16)) / l).astype(o_ref.dtype)


def _mem_decode(mq_s, mem_k, mem_v):
    nb, nrow, d = mq_s.shape
    nkey = mem_k.shape[1]
    nseq = min(MEM_DECODE_SEQS_PER_STEP, nb)
    return pl.pallas_call(
        _mem_decode_kernel,
        grid=(nb // nseq,),
        in_specs=[pl.BlockSpec((nseq, nrow, d), lambda b: (b, 0, 0)),
                  pl.BlockSpec((nseq, nkey, d), lambda b: (b, 0, 0)),
                  pl.BlockSpec((nseq, nkey, d), lambda b: (b, 0, 0))],
        out_specs=pl.BlockSpec((nseq, nrow, d), lambda b: (b, 0, 0)),
        out_shape=jax.ShapeDtypeStruct((nb, nrow, d), BF16),
        compiler_params=pltpu.CompilerParams(dimension_semantics=("parallel",)),
        name="mem_decode_attn",
    )(mq_s, mem_k, mem_v)


def _rope_tables(pos):
    half = ROPE // 2
    inv = jnp.exp(-math.log(ROPE_THETA) * jnp.arange(half, dtype=F32) / half)
    ang = pos.astype(F32)[:, None] * inv
    cos, sin = jnp.cos(ang), jnp.sin(ang)
    pad = jnp.zeros((pos.shape[0], LANES - ROPE), F32)
    return (jnp.concatenate([cos, cos, pad], axis=1), jnp.concatenate([-sin, sin, pad], axis=1))


def _prepare_weights(W_in, b_forget, b_gate, q_norm_g, W_uq, kv_norm_g, W_uk, W_uv,
                     W_br_a, W_br_f, W_br_m, W_out, ln1_g, ln1_b, w_gate, w_up, w_down, ln2_g, ln2_b):
    splits = (Q_LORA, KV_LORA, ROPE, H_F * HD_F, H_F * HD_F, H_F * HD_F, H_F, H_M * HD_M, 3 * D_MODEL)
    offs = [0]
    for n in splits:
        offs.append(offs[-1] + n)
    wq, wkv, wkr, wfq, wfk, wfv, wf, wmq, wg = (W_in[:, offs[i]:offs[i + 1]] for i in range(9))
    pad_to = lambda w: jnp.pad(w, ((0, 0), (0, LANES - w.shape[1])))
    wp = {}
    wp["w_front"] = jnp.concatenate([wq, wkv, wfq, wfk, wfv, wmq, pad_to(wkr), pad_to(wf)], axis=1).astype(BF16)
    w_nope = W_uq[:, :, :NOPE].reshape(Q_LORA, H_A * NOPE)
    w_rope = jnp.pad(W_uq[:, :, NOPE:], ((0, 0), (0, 0), (0, LANES - ROPE))).reshape(Q_LORA, H_A * LANES)
    wp["w_uq"] = jnp.concatenate([w_nope, w_rope], axis=1).astype(BF16)
    wp["w_uk"] = jnp.transpose(W_uk, (1, 2, 0)).astype(BF16)
    wp["w_uv"] = jnp.transpose(W_uv, (1, 0, 2)).astype(BF16)
    wp["q_g"] = q_norm_g.reshape(1, Q_LORA)
    wp["kv_g"] = kv_norm_g.reshape(1, KV_LORA)
    wp["b_f"] = jnp.pad(b_forget, (0, LANES - H_F)).reshape(1, LANES)
    wp["w_g"] = wg.astype(BF16)
    wp["b_g"] = b_gate.reshape(1, 3 * D_MODEL)
    wp["w_br_a"] = W_br_a.astype(BF16)
    wp["w_br_f"] = W_br_f.astype(BF16)
    wp["w_br_m"] = W_br_m.astype(BF16)
    wp["w_out"] = W_out.astype(BF16)
    wp["ln1_g"] = ln1_g.reshape(1, D_MODEL)
    wp["ln1_b"] = ln1_b.reshape(1, D_MODEL)
    wp["w_gate"] = w_gate.astype(BF16)
    wp["w_up"] = w_up.astype(BF16)
    wp["w_down"] = w_down.astype(BF16)
    wp["ln2_g"] = ln2_g.reshape(1, D_MODEL)
    wp["ln2_b"] = ln2_b.reshape(1, D_MODEL)
    return wp


def kernel(x_prompt, x_sample, cache_mla_latent, cache_mla_krope, cache_fox_k, cache_fox_v, cache_fox_logf, cache_mem_k, cache_mem_v, page_table, mem_prompt, W_in, b_forget, b_gate, q_norm_g, W_uq, kv_norm_g, W_uk, W_uv, W_mem_k, W_mem_v, W_br_a, W_br_f, W_br_m, W_out, ln1_g, ln1_b, w_gate, w_up, w_down, ln2_g, ln2_b):
    wp = _prepare_weights(W_in, b_forget, b_gate, q_norm_g, W_uq, kv_norm_g, W_uk, W_uv,
                          W_br_a, W_br_f, W_br_m, W_out, ln1_g, ln1_b, w_gate, w_up, w_down, ln2_g, ln2_b)
    bp, sp, _ = x_prompt.shape
    bs, ss, _ = x_sample.shape
    n_pool, page, _ = cache_mla_latent.shape
    n_mem = mem_prompt.shape[1]
    past_len = page_table.shape[1] * page

    xp = x_prompt.reshape(bp * sp, D_MODEL)
    cos_p, sin_p = _rope_tables(jnp.arange(sp, dtype=jnp.int32))
    (lat_p, krope_p, fk_p, fv_p, logf_p, qcat_p, kcat_p, fq_p, fk16_p, fv16_p, mq_p) = _front(xp, cos_p, sin_p, wp)
    w_mem = jnp.concatenate([W_mem_k.reshape(D_MODEL, H_M * HD_M),
                             W_mem_v.reshape(D_MODEL, H_M * HD_M)], axis=1).astype(BF16)
    mem_kv = _matmul(mem_prompt.reshape(bp * n_mem, D_MODEL), w_mem)
    mem_k_p = mem_kv[:, :H_M * HD_M].reshape(bp, n_mem, H_M * HD_M)
    mem_v_p = mem_kv[:, H_M * HD_M:].reshape(bp, n_mem, H_M * HD_M)

    logf_t = jnp.transpose(logf_p.reshape(bp, sp, H_F), (0, 2, 1))
    c = _cumsum_lanes(logf_t.reshape(bp * H_F, sp))
    c_row = c.reshape(bp, H_F // 2, 2, sp)
    c_col = jnp.broadcast_to(c.reshape(bp, H_F, sp, 1), (bp, H_F, sp, LANES))

    o_lat_p = _mla_flash(qcat_p.reshape(bp, sp, H_A * QCAT), kcat_p.reshape(bp, sp, QCAT))
    o_f_p = _fox_flash(fq_p.reshape(bp, sp, -1), fk16_p.reshape(bp, sp, -1), fv16_p.reshape(bp, sp, -1),
                       c_col, c_row)
    o_m_p = _mem_attn(mq_p.reshape(bp, sp, -1), mem_k_p, mem_v_p)
    h_p = _merge(xp, o_lat_p.reshape(bp * sp, -1), o_f_p.reshape(bp * sp, -1), o_m_p.reshape(bp * sp, -1), wp)
    y_p = _mlp(h_p, wp)

    xs = x_sample.reshape(bs * ss, D_MODEL)
    tm_s = min(ROW_TILE, bs * ss)
    pos_s = past_len + (jnp.arange(tm_s, dtype=jnp.int32) % ss)
    cos_s, sin_s = _rope_tables(pos_s)
    (lat_s, krope_s, fk_s, fv_s, logf_s, qcat_s, _, fq_s, _, _, mq_s) = _front(xs, cos_s, sin_s, wp)
    pool_lf_t = _pool_suffix(jnp.transpose(cache_fox_logf, (0, 2, 1)))
    pool_kr_t = jnp.transpose(cache_mla_krope, (0, 2, 1))
    pool_fk_t = jnp.transpose(cache_fox_k, (0, 2, 3, 1)).reshape(n_pool, H_F * HD_F, page)
    pool_fv_t = jnp.transpose(cache_fox_v, (0, 2, 3, 1)).reshape(n_pool, H_F * HD_F, page)
    o_lat_s, o_f_s = _decode_attn(
        page_table, qcat_s.reshape(bs, ss * H_A, QCAT), fq_s.reshape(bs, ss, -1),
        lat_s.reshape(bs, ss, -1), krope_s.reshape(bs, ss, -1), fk_s.reshape(bs, ss, -1),
        fv_s.reshape(bs, ss, -1), logf_s.reshape(bs, ss, -1),
        cache_mla_latent, pool_kr_t, pool_fk_t, pool_fv_t, pool_lf_t)
    o_m_s = _mem_decode(mq_s.reshape(bs, ss * H_M, HD_M), cache_mem_k.reshape(bs, n_mem * H_M, HD_M),
                        cache_mem_v.reshape(bs, n_mem * H_M, HD_M))
    h_s = _merge(xs, o_lat_s.reshape(bs * ss, -1), o_f_s.reshape(bs * ss, -1), o_m_s.reshape(bs * ss, -1), wp)
    y_s = _mlp(h_s, wp)

    return (y_p.reshape(bp, sp, D_MODEL), y_s.reshape(bs, ss, D_MODEL),
            lat_p.reshape(bp, sp, KV_LORA), krope_p.reshape(bp, sp, ROPE),
            fk_p.reshape(bp, sp, H_F, HD_F), fv_p.reshape(bp, sp, H_F, HD_F), logf_p.reshape(bp, sp, H_F),
            mem_k_p.reshape(bp, n_mem, H_M, HD_M), mem_v_p.reshape(bp, n_mem, H_M, HD_M),
            lat_s.reshape(bs, ss, KV_LORA), krope_s.reshape(bs, ss, ROPE),
            fk_s.reshape(bs, ss, H_F, HD_F), fv_s.reshape(bs, ss, H_F, HD_F), logf_s.reshape(bs, ss, H_F))
```

```python
import functools
import math

import jax
import jax.numpy as jnp
from jax import lax
from jax.experimental import pallas as pl
from jax.experimental.pallas import tpu as pltpu

F32 = jnp.float32
BF16 = jnp.bfloat16

H_A, NOPE, ROPE, V_A = 8, 128, 64, 128
Q_LORA, KV_LORA = 512, 256
H_F, HD_F = 8, 64
H_M, HD_M = 4, 128
D_MODEL = 1024
ROPE_THETA = 10000.0
LN_EPS = 1e-5
RMS_EPS = 1e-6
DEPTH = 1
ALPHA = (2 * DEPTH) ** 0.25
MLA_SCALE = (NOPE + ROPE) ** -0.5
FOX_SCALE = HD_F ** -0.5
MEM_SCALE = HD_M ** -0.5

LANES = 128
QCAT = KV_LORA + LANES
NEG_INF = float("-inf")

ROW_TILE = 256
ATT_TILE = 512
SOFTMAX_ROWS = 32
MLA_HEADS_PER_STEP = 4
FOX_PAIRS_PER_STEP = 2
MEM_DECODE_SEQS_PER_STEP = 4
CUMSUM_CHUNK = 256
DECODE_PAGES_PER_GROUP = 8
POOL_PAGES_PER_STEP = 256
MLP_UP_PIECES = 5
MLP_DOWN_PIECES = 3


def _dot(a, b):
    return jnp.dot(a, b, preferred_element_type=F32)


def _dot_nt(a, b):
    return lax.dot_general(a, b, (((1,), (1,)), ((), ())), preferred_element_type=F32)


def _const_spec(shape):
    zeros = (0,) * len(shape)
    return pl.BlockSpec(shape, lambda *_: zeros, pipeline_mode=pl.Buffered(1))


def _split3(x):
    hi = x.astype(BF16).astype(F32)
    r = x - hi
    mid = r.astype(BF16).astype(F32)
    lo = (r - mid).astype(BF16).astype(F32)
    return hi, mid, lo


def _exact_dot_ones(x, tri):
    hi, mid, lo = _split3(x)
    return _dot(hi, tri) + _dot(mid, tri) + _dot(lo, tri)


def _layernorm(v, g, b):
    mu = jnp.mean(v, axis=-1, keepdims=True)
    d = v - mu
    var = jnp.mean(d * d, axis=-1, keepdims=True)
    return d * lax.rsqrt(var + LN_EPS) * g + b


def _rmsnorm(v, g):
    return v * lax.rsqrt(jnp.mean(v * v, axis=-1, keepdims=True) + RMS_EPS) * g


def _sigmoid(v):
    return 1.0 / (1.0 + jnp.exp(-v))


def _rope_tile(x, cos, sin_signed):
    lane = lax.broadcasted_iota(jnp.int32, x.shape, 1)
    rot = jnp.where(lane < ROPE // 2, pltpu.roll(x, LANES - ROPE // 2, 1), pltpu.roll(x, ROPE // 2, 1))
    return x * cos + rot * sin_signed


def _front_kernel(x_ref, cos_ref, sin_ref, w_ref, wuq_ref, wuk_ref, qg_ref, kvg_ref, bf_ref,
                  lat_ref, krope_ref, fk_ref, fv_ref, logf_ref,
                  qcat_ref, kcat_ref, fq_ref, fk16_ref, fv16_ref, mq_ref):
    xb = x_ref[...].astype(BF16)
    z = _dot(xb, w_ref[...])
    cos = cos_ref[...]
    sin = sin_ref[...]
    qn = _rmsnorm(z[:, 0:512], qg_ref[...]).astype(BF16)
    q = _dot(qn, wuq_ref[...])
    for h in range(H_A):
        q_nope = q[:, h * NOPE:(h + 1) * NOPE].astype(BF16)
        qcat_ref[:, h * QCAT:h * QCAT + KV_LORA] = _dot(q_nope, wuk_ref[h]).astype(BF16)
        xr = q[:, H_A * NOPE + h * LANES:H_A * NOPE + (h + 1) * LANES]
        qcat_ref[:, h * QCAT + KV_LORA:(h + 1) * QCAT] = _rope_tile(xr, cos, sin).astype(BF16)
    lat = _rmsnorm(z[:, 512:768], kvg_ref[...])
    lat_ref[...] = lat
    kcat_ref[:, 0:KV_LORA] = lat.astype(BF16)
    kr = _rope_tile(z[:, 2816:2944], cos, sin)
    krope_ref[...] = kr[:, 0:ROPE]
    kcat_ref[:, KV_LORA:QCAT] = kr.astype(BF16)
    fq_ref[...] = (z[:, 768:1280] * FOX_SCALE).astype(BF16)
    zfk = z[:, 1280:1792]
    zfv = z[:, 1792:2304]
    fk_ref[...] = zfk
    fv_ref[...] = zfv
    fk16_ref[...] = zfk.astype(BF16)
    fv16_ref[...] = zfv.astype(BF16)
    mq_ref[...] = z[:, 2304:2816].astype(BF16)
    zf = z[:, 2944:3072] + bf_ref[...]
    logf = jnp.minimum(zf, 0.0) - jnp.log1p(jnp.exp(-jnp.abs(zf)))
    logf_ref[...] = logf[:, 0:H_F]


def _front(x2d, cos_t, sin_t, wp):
    t = x2d.shape[0]
    tm = min(ROW_TILE, t)
    nblk = t // tm
    ntab = cos_t.shape[0] // tm
    row = lambda n: pl.BlockSpec((tm, n), lambda i: (i, 0))
    tab = pl.BlockSpec((tm, LANES), lambda i: (i % ntab, 0))
    out_shape = (
        jax.ShapeDtypeStruct((t, KV_LORA), F32), jax.ShapeDtypeStruct((t, ROPE), F32),
        jax.ShapeDtypeStruct((t, H_F * HD_F), F32), jax.ShapeDtypeStruct((t, H_F * HD_F), F32),
        jax.ShapeDtypeStruct((t, H_F), F32),
        jax.ShapeDtypeStruct((t, H_A * QCAT), BF16), jax.ShapeDtypeStruct((t, QCAT), BF16),
        jax.ShapeDtypeStruct((t, H_F * HD_F), BF16), jax.ShapeDtypeStruct((t, H_F * HD_F), BF16),
        jax.ShapeDtypeStruct((t, H_F * HD_F), BF16), jax.ShapeDtypeStruct((t, H_M * HD_M), BF16),
    )
    out_specs = (row(KV_LORA), row(ROPE), row(512), row(512), row(H_F),
                 row(H_A * QCAT), row(QCAT), row(512), row(512), row(512), row(512))
    return pl.pallas_call(
        _front_kernel,
        grid=(nblk,),
        in_specs=[row(D_MODEL), tab, tab,
                  _const_spec(wp["w_front"].shape), _const_spec(wp["w_uq"].shape),
                  _const_spec(wp["w_uk"].shape), _const_spec((1, Q_LORA)),
                  _const_spec((1, KV_LORA)), _const_spec((1, LANES))],
        out_specs=out_specs,
        out_shape=out_shape,
        compiler_params=pltpu.CompilerParams(dimension_semantics=("parallel",)),
        name="front",
    )(x2d, cos_t, sin_t, wp["w_front"], wp["w_uq"], wp["w_uk"], wp["q_g"], wp["kv_g"], wp["b_f"])


def _mm_kernel(x_ref, w_ref, o_ref):
    o_ref[...] = _dot(x_ref[...].astype(BF16), w_ref[...])


def _matmul(x2d, w):
    t, k = x2d.shape
    n = w.shape[1]
    tm = min(ROW_TILE, t)
    return pl.pallas_call(
        _mm_kernel,
        grid=(t // tm,),
        in_specs=[pl.BlockSpec((tm, k), lambda i: (i, 0)), _const_spec((k, n))],
        out_specs=pl.BlockSpec((tm, n), lambda i: (i, 0)),
        out_shape=jax.ShapeDtypeStruct((t, n), F32),
        compiler_params=pltpu.CompilerParams(dimension_semantics=("parallel",)),
        name="mem_kv_proj",
    )(x2d, w)


def _cumsum_kernel(x_ref, o_ref):
    n = x_ref.shape[1]
    ck = min(CUMSUM_CHUNK, n)
    r = lax.broadcasted_iota(jnp.int32, (ck, ck), 0)
    c = lax.broadcasted_iota(jnp.int32, (ck, ck), 1)
    tri = (r <= c).astype(F32)
    carry = jnp.zeros((x_ref.shape[0], 1), F32)
    for j in range(n // ck):
        cs = _exact_dot_ones(x_ref[:, j * ck:(j + 1) * ck], tri) + carry
        o_ref[:, j * ck:(j + 1) * ck] = cs
        carry = cs[:, ck - 1:ck]


def _cumsum_lanes(x2d):
    return pl.pallas_call(
        _cumsum_kernel,
        out_shape=jax.ShapeDtypeStruct(x2d.shape, F32),
        name="logf_cumsum",
    )(x2d)


def _lane_tile(x, width):
    n = width // LANES
    return x if n == 1 else jnp.concatenate([x] * n, axis=1)


def _softmax_rows(s_ref, p_ref, m_ref, l_ref, a_ref, idx, logit_fn, diagonal):
    tq, tk = s_ref.shape[1:]
    assert not diagonal or tq == tk
    rc = min(SOFTMAX_ROWS, tq)
    for c in range(tq // rc):
        rows = slice(c * rc, (c + 1) * rc)
        w = min(tk, pl.cdiv((c + 1) * rc, LANES) * LANES) if diagonal else tk
        s = logit_fn(s_ref[idx, rows, 0:w], rows, w)
        if diagonal:
            qpos = c * rc + lax.broadcasted_iota(jnp.int32, (rc, w), 0)
            kpos = lax.broadcasted_iota(jnp.int32, (rc, w), 1)
            s = jnp.where(kpos <= qpos, s, NEG_INF)
        m_prev = m_ref[idx, rows, :]
        m_new = jnp.maximum(m_prev, jnp.max(s, axis=-1, keepdims=True))
        alpha = jnp.exp(m_prev - m_new)
        p = jnp.exp(s - _lane_tile(m_new, w))
        l_ref[idx, rows, :] = alpha * l_ref[idx, rows, :] + jnp.sum(p, axis=-1, keepdims=True)
        m_ref[idx, rows, :] = m_new
        a_ref[idx, rows, :] = alpha
        p_ref[idx, rows, 0:w] = p.astype(BF16)
        if w < tk:
            p_ref[idx, rows, w:tk] = jnp.zeros((rc, tk - w), BF16)


def _flash_init(ki, m_ref, l_ref, acc_ref):
    @pl.when(ki == 0)
    def _():
        m_ref[...] = jnp.full(m_ref.shape, NEG_INF, F32)
        l_ref[...] = jnp.zeros(l_ref.shape, F32)
        acc_ref[...] = jnp.zeros(acc_ref.shape, F32)


def _causal_steps(n):
    pairs = [(qi, ki) for qi in range(n) for ki in range(qi + 1)]
    return (jnp.array([p[0] for p in pairs], jnp.int32), jnp.array([p[1] for p in pairs], jnp.int32))


def _flash_scratch(nh, t, width):
    return [pltpu.VMEM((nh, t, t), F32), pltpu.VMEM((nh, t, t), BF16),
            pltpu.VMEM((nh, t, LANES), F32), pltpu.VMEM((nh, t, LANES), F32),
            pltpu.VMEM((nh, t, LANES), F32), pltpu.VMEM((nh, t, width), F32)]


def _mla_flash_kernel(qt_ref, kt_ref, q_ref, k_ref, o_ref, s_ref, p_ref, m_ref, l_ref, a_ref, acc_ref):
    step_id = pl.program_id(2)
    qi = qt_ref[step_id]
    ki = kt_ref[step_id]
    tq = q_ref.shape[1]
    tk = k_ref.shape[1]
    nh = s_ref.shape[0]
    _flash_init(ki, m_ref, l_ref, acc_ref)

    def step(masked):
        k = k_ref[0]
        for g in range(nh):
            s_ref[g] = _dot_nt(q_ref[0, :, g * QCAT:(g + 1) * QCAT], k)
        for g in range(nh):
            _softmax_rows(s_ref, p_ref, m_ref, l_ref, a_ref, g, lambda s, rows, w: s * MLA_SCALE, masked)
            acc_ref[g] = _lane_tile(a_ref[g], KV_LORA) * acc_ref[g] + _dot(p_ref[g], k[:, 0:KV_LORA])

    @pl.when(ki < qi)
    def _():
        step(False)

    @pl.when(ki == qi)
    def _():
        step(True)
        for g in range(nh):
            o_ref[0, :, g * KV_LORA:(g + 1) * KV_LORA] = (
                acc_ref[g] / _lane_tile(l_ref[g], KV_LORA)).astype(o_ref.dtype)


def _mla_flash(qcat, kcat):
    b, s, _ = qcat.shape
    t = min(ATT_TILE, s)
    qt, kt = _causal_steps(s // t)
    hg = MLA_HEADS_PER_STEP
    grid_spec = pltpu.PrefetchScalarGridSpec(
        num_scalar_prefetch=2,
        grid=(b, H_A // hg, qt.shape[0]),
        in_specs=[pl.BlockSpec((1, t, hg * QCAT), lambda b, h, i, qt, kt: (b, qt[i], h)),
                  pl.BlockSpec((1, t, QCAT), lambda b, h, i, qt, kt: (b, kt[i], 0))],
        out_specs=pl.BlockSpec((1, t, hg * KV_LORA), lambda b, h, i, qt, kt: (b, qt[i], h)),
        scratch_shapes=_flash_scratch(hg, t, KV_LORA),
    )
    return pl.pallas_call(
        _mla_flash_kernel,
        grid_spec=grid_spec,
        out_shape=jax.ShapeDtypeStruct((b, s, H_A * KV_LORA), BF16),
        compiler_params=pltpu.CompilerParams(dimension_semantics=("parallel", "parallel", "arbitrary")),
        name="mla_prompt_attn",
    )(qt, kt, qcat, kcat)


def _fox_flash_kernel(qt_ref, kt_ref, q_ref, k_ref, v_ref, cq_ref, ck_ref, o_ref,
                      s_ref, p_ref, m_ref, l_ref, a_ref, acc_ref):
    step_id = pl.program_id(2)
    qi = qt_ref[step_id]
    ki = kt_ref[step_id]
    tq = q_ref.shape[1]
    tk = k_ref.shape[1]
    npair = s_ref.shape[0] // 2
    _flash_init(ki, m_ref, l_ref, acc_ref)

    def step(masked):
        lane = lax.broadcasted_iota(jnp.int32, (tq, LANES), 1)
        for pr in range(npair):
            q2 = q_ref[0, :, pr * LANES:(pr + 1) * LANES]
            k2 = k_ref[0, :, pr * LANES:(pr + 1) * LANES]
            for hh in range(2):
                keep = (lane >= HD_F) if hh else (lane < HD_F)
                s_ref[2 * pr + hh] = _dot_nt(jnp.where(keep, q2, jnp.zeros_like(q2)), k2)
        for pr in range(npair):
            v2 = v_ref[0, :, pr * LANES:(pr + 1) * LANES]
            for hh in range(2):
                h = 2 * pr + hh

                def logits(s, rows, w, h=h, pr=pr, hh=hh):
                    return s + _lane_tile(cq_ref[0, h, rows, :], w) - ck_ref[0, pr, hh:hh + 1, 0:w]
                _softmax_rows(s_ref, p_ref, m_ref, l_ref, a_ref, h, logits, masked)
                acc_ref[h] = a_ref[h] * acc_ref[h] + _dot(p_ref[h], v2)

    @pl.when(ki < qi)
    def _():
        step(False)

    @pl.when(ki == qi)
    def _():
        step(True)
        lane = lax.broadcasted_iota(jnp.int32, (tq, LANES), 1)
        for pr in range(npair):
            o0 = acc_ref[2 * pr] / l_ref[2 * pr]
            o1 = acc_ref[2 * pr + 1] / l_ref[2 * pr + 1]
            o_ref[0, :, pr * LANES:(pr + 1) * LANES] = jnp.where(lane < HD_F, o0, o1).astype(o_ref.dtype)


def _fox_flash(fq, fk16, fv16, c_col, c_row):
    b, s, _ = fq.shape
    t = min(ATT_TILE, s)
    qt, kt = _causal_steps(s // t)
    pp = FOX_PAIRS_PER_STEP
    qspec = pl.BlockSpec((1, t, pp * LANES), lambda b, h, i, qt, kt: (b, qt[i], h))
    kspec = pl.BlockSpec((1, t, pp * LANES), lambda b, h, i, qt, kt: (b, kt[i], h))
    grid_spec = pltpu.PrefetchScalarGridSpec(
        num_scalar_prefetch=2,
        grid=(b, H_F // (2 * pp), qt.shape[0]),
        in_specs=[qspec, kspec, kspec,
                  pl.BlockSpec((1, 2 * pp, t, LANES), lambda b, h, i, qt, kt: (b, h, qt[i], 0)),
                  pl.BlockSpec((1, pp, 2, t), lambda b, h, i, qt, kt: (b, h, 0, kt[i]))],
        out_specs=qspec,
        scratch_shapes=_flash_scratch(2 * pp, t, LANES),
    )
    return pl.pallas_call(
        _fox_flash_kernel,
        grid_spec=grid_spec,
        out_shape=jax.ShapeDtypeStruct((b, s, H_F * HD_F), BF16),
        compiler_params=pltpu.CompilerParams(dimension_semantics=("parallel", "parallel", "arbitrary")),
        name="fox_prompt_attn",
    )(qt, kt, fq, fk16, fv16, c_col, c_row)


def _mem_attn_kernel(q_ref, k_ref, v_ref, o_ref, s_ref, p_ref, l_ref):
    tq = q_ref.shape[1]
    rc = min(SOFTMAX_ROWS, tq)
    heads = [slice(h * HD_M, (h + 1) * HD_M) for h in range(H_M)]
    for h, hs in enumerate(heads):
        s_ref[h] = _dot_nt(q_ref[0, :, hs], k_ref[0, :, hs].astype(BF16))
    for h, hs in enumerate(heads):
        for c in range(tq // rc):
            rows = slice(c * rc, (c + 1) * rc)
            s = s_ref[h, rows, :] * MEM_SCALE
            p = jnp.exp(s - jnp.max(s, axis=-1, keepdims=True))
            l_ref[h, rows, :] = jnp.broadcast_to(jnp.sum(p, axis=-1, keepdims=True), (rc, HD_M))
            p_ref[h, rows, :] = p.astype(BF16)
        o_ref[0, :, hs] = (_dot(p_ref[h], v_ref[0, :, hs].astype(BF16)) / l_ref[h]).astype(o_ref.dtype)


def _mem_attn(mq, mem_k, mem_v):
    b, s, d = mq.shape
    nm = mem_k.shape[1]
    t = min(ATT_TILE, s)
    return pl.pallas_call(
        _mem_attn_kernel,
        grid=(b, s // t),
        in_specs=[pl.BlockSpec((1, t, d), lambda b, qi: (b, qi, 0)),
                  pl.BlockSpec((1, nm, d), lambda b, qi: (b, 0, 0)),
                  pl.BlockSpec((1, nm, d), lambda b, qi: (b, 0, 0))],
        out_specs=pl.BlockSpec((1, t, d), lambda b, qi: (b, qi, 0)),
        out_shape=jax.ShapeDtypeStruct((b, s, d), BF16),
        scratch_shapes=[pltpu.VMEM((H_M, t, nm), F32), pltpu.VMEM((H_M, t, nm), BF16),
                        pltpu.VMEM((H_M, t, HD_M), F32)],
        compiler_params=pltpu.CompilerParams(dimension_semantics=("parallel", "parallel")),
        name="mem_prompt_attn",
    )(mq, mem_k, mem_v)


def _merge_kernel(x_ref, olat_ref, of_ref, om_ref, wg_ref, bg_ref, wuv_ref, wa_ref, wf_ref, wm_ref,
                  wo_ref, g_ref, b_ref, h_ref, oa_ref):
    x = x_ref[...]
    xb = x.astype(BF16)
    for h in range(H_A):
        oa_ref[:, h * V_A:(h + 1) * V_A] = _dot(
            olat_ref[:, h * KV_LORA:(h + 1) * KV_LORA], wuv_ref[h]).astype(BF16)
    branches = (_dot(oa_ref[...], wa_ref[...]), _dot(of_ref[...], wf_ref[...]),
                _dot(om_ref[...], wm_ref[...]))
    merged = None
    for i, br in enumerate(branches):
        gate = _sigmoid(_dot(xb, wg_ref[:, i * D_MODEL:(i + 1) * D_MODEL])
                        + bg_ref[:, i * D_MODEL:(i + 1) * D_MODEL])
        merged = gate * br if merged is None else merged + gate * br
    pre = ALPHA * x + _dot(merged.astype(BF16), wo_ref[...])
    h_ref[...] = _layernorm(pre, g_ref[...], b_ref[...])


def _merge(x2d, o_lat, o_f, o_m, wp):
    t = x2d.shape[0]
    tm = min(ROW_TILE, t)
    row = lambda n: pl.BlockSpec((tm, n), lambda i: (i, 0))
    names = ("w_g", "b_g", "w_uv", "w_br_a", "w_br_f", "w_br_m", "w_out", "ln1_g", "ln1_b")
    return pl.pallas_call(
        _merge_kernel,
        grid=(t // tm,),
        in_specs=[row(D_MODEL), row(H_A * KV_LORA), row(H_F * HD_F), row(H_M * HD_M)]
        + [_const_spec(wp[n].shape) for n in names],
        out_specs=row(D_MODEL),
        out_shape=jax.ShapeDtypeStruct((t, D_MODEL), F32),
        scratch_shapes=[pltpu.VMEM((tm, H_A * V_A), BF16)],
        compiler_params=pltpu.CompilerParams(dimension_semantics=("parallel",)),
        name="merge_ln1",
    )(x2d, o_lat, o_f, o_m, *[wp[n] for n in names])


def _mlp_kernel(h_ref, wg_ref, wu_ref, wd_ref, g_ref, b_ref, y_ref, a_ref, *, n_chunks):
    h = h_ref[...]
    hb = h.astype(BF16)
    fc = wg_ref.shape[1] // n_chunks
    for c in range(n_chunks):
        gt = _dot(hb, wg_ref[:, c * fc:(c + 1) * fc])
        up = _dot(hb, wu_ref[:, c * fc:(c + 1) * fc])
        a_ref[:, c * fc:(c + 1) * fc] = (gt * _sigmoid(gt) * up).astype(BF16)
    y_ref[...] = _layernorm(ALPHA * h + _dot(a_ref[...], wd_ref[...]), g_ref[...], b_ref[...])


def _lane_tile_chunks(width, parts):
    tiles = width // LANES
    bounds = [LANES * ((i * tiles) // parts) for i in range(parts + 1)]
    return [(bounds[i], bounds[i + 1]) for i in range(parts) if bounds[i + 1] > bounds[i]]


def _mlp_work_items(h_ref, wg_ref, wu_ref, wd_ref, g_ref, b_ref, y_ref, a_ref, pre_ref):
    def up_piece(c0, c1):
        hb = h_ref[...].astype(BF16)
        gt = _dot(hb, wg_ref[:, c0:c1])
        up = _dot(hb, wu_ref[:, c0:c1])
        a_ref[:, c0:c1] = (gt * _sigmoid(gt) * up).astype(BF16)

    def down_piece(c0, c1, last):
        pre_ref[:, c0:c1] = _dot(a_ref[...], wd_ref[:, c0:c1])
        if last:
            y_ref[...] = _layernorm(ALPHA * h_ref[...] + pre_ref[...], g_ref[...], b_ref[...])

    items = [functools.partial(up_piece, c0, c1)
             for c0, c1 in _lane_tile_chunks(wg_ref.shape[1], MLP_UP_PIECES)]
    down = _lane_tile_chunks(wd_ref.shape[1], MLP_DOWN_PIECES)
    items += [functools.partial(down_piece, c0, c1, i == len(down) - 1) for i, (c0, c1) in enumerate(down)]
    return items


def _mlp(h2d, wp):
    t = h2d.shape[0]
    tm = min(ROW_TILE, t)
    d_ff = wp["w_gate"].shape[1]
    n_chunks = 2
    row = lambda n: pl.BlockSpec((tm, n), lambda i: (i, 0))
    names = ("w_gate", "w_up", "w_down", "ln2_g", "ln2_b")
    return pl.pallas_call(
        functools.partial(_mlp_kernel, n_chunks=n_chunks),
        grid=(t // tm,),
        in_specs=[row(D_MODEL)] + [_const_spec(wp[n].shape) for n in names],
        out_specs=row(D_MODEL),
        out_shape=jax.ShapeDtypeStruct((t, D_MODEL), F32),
        scratch_shapes=[pltpu.VMEM((tm, d_ff), BF16)],
        compiler_params=pltpu.CompilerParams(dimension_semantics=("parallel",)),
        name="mlp_ln2",
    )(h2d, *[wp[n] for n in names])


def _rows_per_query(x, nq):
    return jnp.concatenate([jnp.broadcast_to(x[q:q + 1], (8, x.shape[1])) for q in range(nq)], axis=0)


def _page_copies(pt_ref, pools, bufs, sem, b, group, slot, pg, npages):
    copies = []
    for i in range(pg):
        pid = pt_ref[b * npages + group * pg + i]
        for a, (pool, buf) in enumerate(zip(pools, bufs)):
            copy = pltpu.make_async_copy(pool.at[pid], buf.at[slot, i], sem.at[slot, a])
            copies.append((copy, i % 2))
    return copies


def _pool_suffix_kernel(x_ref, o_ref):
    pb = x_ref.shape[0]
    l = x_ref[...].reshape(pb * H_F, LANES)
    r = lax.broadcasted_iota(jnp.int32, (LANES, LANES), 0)
    c = lax.broadcasted_iota(jnp.int32, (LANES, LANES), 1)
    excl = _exact_dot_ones(l, (r > c).astype(F32))
    tot = jnp.broadcast_to(excl[:, 0:1] + l[:, 0:1], excl.shape)
    o_ref[:, 0:H_F, :] = excl.reshape(pb, H_F, LANES)
    o_ref[:, H_F:2 * H_F, :] = tot.reshape(pb, H_F, LANES)


def _pool_suffix(logf_t):
    n_pool = logf_t.shape[0]
    pb = min(POOL_PAGES_PER_STEP, n_pool)
    return pl.pallas_call(
        _pool_suffix_kernel,
        grid=(n_pool // pb,),
        in_specs=[pl.BlockSpec((pb, H_F, LANES), lambda i: (i, 0, 0))],
        out_specs=pl.BlockSpec((pb, 2 * H_F, LANES), lambda i: (i, 0, 0)),
        out_shape=jax.ShapeDtypeStruct((n_pool, 2 * H_F, LANES), F32),
        compiler_params=pltpu.CompilerParams(dimension_semantics=("parallel",)),
        name="pool_logf_suffix",
    )(logf_t)


def _decode_kernel(pt_ref, qcat_ref, fq_ref, latn_ref, krn_ref, fkn_ref, fvn_ref, lfn_ref,
                   lat_hbm, kr_hbm, fk_hbm, fv_hbm, lf_hbm,
                   h_ref, wg_ref, wu_ref, wd_ref, g2_ref, b2_ref,
                   olat_ref, of_ref, y_ref,
                   lat_buf, kr_buf, fk_buf, fv_buf, lf_buf, sem,
                   latc_ref, krc_ref, fkc_ref, fvc_ref, pm_ref, pl_ref, pacca_ref, paccf_ref,
                   a_ref, pre_ref, *, pg, ng, nq, npages):
    mlp_items = _mlp_work_items(h_ref, wg_ref, wu_ref, wd_ref, g2_ref, b2_ref, y_ref, a_ref, pre_ref)
    b = pl.program_id(0)
    nb = pl.num_programs(0)
    nrow = nq * 8
    page = lat_buf.shape[2]
    pools = (lat_hbm, kr_hbm, fk_hbm, fv_hbm, lf_hbm)
    bufs = (lat_buf, kr_buf, fk_buf, fv_buf, lf_buf)
    copies = functools.partial(_page_copies, pt_ref, pools, bufs, sem, pg=pg, npages=npages)

    @pl.when(b == 0)
    def _():
        for c, prio in copies(b, ng - 1, 0):
            c.start(priority=prio)

    row512 = lax.broadcasted_iota(jnp.int32, (nrow, H_F * HD_F), 0)
    lane512 = lax.broadcasted_iota(jnp.int32, (nrow, H_F * HD_F), 1)
    head_mask = (lane512 // HD_F) == (row512 % 8)

    def store_partial(idx, k, m, l, acc, acc_ref):
        pm_ref[idx, k] = jnp.broadcast_to(m, (nrow, LANES))
        pl_ref[idx, k] = jnp.broadcast_to(l, (nrow, LANES))
        acc_ref[k] = acc

    qrow = lax.broadcasted_iota(jnp.int32, (nrow, 1), 0) // 8
    qbd = jnp.where(head_mask, _rows_per_query(fq_ref[0].astype(F32), nq), 0.0)
    qbd16 = qbd.astype(BF16)
    lfn = lfn_ref[0]
    r8 = lax.broadcasted_iota(jnp.int32, (nrow, H_F), 0)
    l8 = lax.broadcasted_iota(jnp.int32, (nrow, H_F), 1)
    pick = l8 == (r8 % 8)
    cum = []
    for q in range(nq):
        cum.append(lfn[q:q + 1] if q == 0 else cum[-1] + lfn[q:q + 1])
    cq = jnp.sum(jnp.where(pick, jnp.concatenate(
        [jnp.broadcast_to(c, (8, H_F)) for c in cum], axis=0), 0.0), axis=1, keepdims=True)
    fkn = fkn_ref[0]
    fvn = fvn_ref[0]
    sf = []
    for s in range(nq):
        ck = jnp.sum(jnp.where(pick, jnp.broadcast_to(cum[s], (nrow, H_F)), 0.0),
                     axis=1, keepdims=True)
        v = jnp.sum(qbd * fkn[s:s + 1], axis=1, keepdims=True) + cq - ck
        sf.append(jnp.where(qrow >= s, v, NEG_INF))
    qc16 = qcat_ref[0]
    qc = qc16.astype(F32)
    latn = latn_ref[0]
    krn = krn_ref[0]
    sa = []
    for s in range(nq):
        v = (jnp.sum(qc[:, 0:KV_LORA] * latn[s:s + 1], axis=1, keepdims=True)
             + jnp.sum(qc[:, KV_LORA:KV_LORA + ROPE] * krn[s:s + 1], axis=1, keepdims=True))
        sa.append(jnp.where(qrow >= s, v * MLA_SCALE, NEG_INF))
    for idx, (sc, vals, acc_ref) in enumerate(((sa, latn, pacca_ref), (sf, fvn, paccf_ref))):
        m = sc[0]
        for s in range(1, nq):
            m = jnp.maximum(m, sc[s])
        l = jnp.zeros_like(m)
        acc = jnp.zeros(acc_ref.shape[1:], F32)
        for s in range(nq):
            p = jnp.exp(sc[s] - m)
            l = l + p
            acc = acc + p * vals[s:s + 1]
        store_partial(idx, ng, m, l, acc, acc_ref)

    carry = jnp.zeros((H_F, LANES), F32)
    for k in range(ng):
        slot = k % 2
        group = ng - 1 - k
        if k + 1 < ng:
            for c, prio in copies(b, group - 1, 1 - slot):
                c.start(priority=prio)
        else:
            @pl.when(b + 1 < nb)
            def _():
                for c, prio in copies(b + 1, ng - 1, 1 - slot):
                    c.start(priority=prio)
        for j, item in enumerate(mlp_items):
            if (j * ng) // len(mlp_items) == k:
                item()
        for c, _ in copies(b, group, slot):
            c.wait()

        for i in range(pg):
            sl = slice(i * page, (i + 1) * page)
            latc_ref[slot, sl, :] = lat_buf[slot, i].astype(BF16)
            krc_ref[slot, :, sl] = kr_buf[slot, i].astype(BF16)
            fkc_ref[slot, :, sl] = fk_buf[slot, i].astype(BF16)
            fvc_ref[slot, :, sl] = fv_buf[slot, i].astype(BF16)

        pieces = [None] * pg
        for i in reversed(range(pg)):
            sums = lf_buf[slot, i]
            pieces[i] = sums[0:H_F] + carry
            carry = carry + sums[H_F:2 * H_F]
        bias = jnp.concatenate(pieces, axis=1)
        bias = jnp.concatenate([bias] * nq, axis=0)

        sf = _dot(qbd16, fkc_ref[slot]) + bias + cq
        m = jnp.max(sf, axis=-1, keepdims=True)
        p = jnp.exp(sf - m)
        store_partial(1, k, m, jnp.sum(p, axis=-1, keepdims=True),
                      _dot_nt(p.astype(BF16), fvc_ref[slot]), paccf_ref)
        latc = latc_ref[slot]
        sa = (_dot_nt(qc16[:, 0:KV_LORA], latc)
              + _dot(qc16[:, KV_LORA:KV_LORA + ROPE], krc_ref[slot])) * MLA_SCALE
        m = jnp.max(sa, axis=-1, keepdims=True)
        p = jnp.exp(sa - m)
        store_partial(0, k, m, jnp.sum(p, axis=-1, keepdims=True), _dot(p.astype(BF16), latc), pacca_ref)

    outs = []
    for idx, acc_ref in enumerate((pacca_ref, paccf_ref)):
        width = acc_ref.shape[2]
        m = pm_ref[idx, 0]
        for k in range(1, ng + 1):
            m = jnp.maximum(m, pm_ref[idx, k])
        l = jnp.zeros((nrow, LANES), F32)
        acc = jnp.zeros((nrow, width), F32)
        for k in range(ng + 1):
            w = jnp.exp(pm_ref[idx, k] - m)
            l = l + w * pl_ref[idx, k]
            acc = acc + _lane_tile(w, width) * acc_ref[k]
        outs.append(acc / _lane_tile(l, width))
    olat_ref[0] = outs[0].astype(olat_ref.dtype)
    of = jnp.where(head_mask, outs[1], 0.0)
    of_ref[0] = jnp.sum(of.reshape(nq, 8, H_F * HD_F), axis=1).astype(of_ref.dtype)


def _decode_attn(page_table, qcat_s, fq_s, lat_s, krope_s, fk_s, fv_s, logf_s,
                 pool_lat, pool_kr, pool_fk, pool_fv, pool_lf, h_prompt, wp):
    nb, npages = page_table.shape
    nq = fq_s.shape[1]
    page = pool_lat.shape[1]
    pg = min(DECODE_PAGES_PER_GROUP, npages)
    ng = npages // pg
    assert ng * pg == npages and ng % 2 == 0, "page groups must alternate between the two buffer slots"
    nrow = nq * 8
    d_f = H_F * HD_F
    d_ff = wp["w_gate"].shape[1]
    t_p = h_prompt.shape[0]
    rows = t_p // nb
    assert rows * nb == t_p and rows % 8 == 0
    pt_flat = page_table.reshape(-1)

    def per_b(shape):
        return pl.BlockSpec((1,) + shape, lambda b, pt: (b,) + (0,) * len(shape))

    hbm = pl.BlockSpec(memory_space=pl.ANY)
    row_block = pl.BlockSpec((rows, D_MODEL), lambda b, pt: (b, 0))
    pools = (pool_lat, pool_kr, pool_fk, pool_fv, pool_lf)
    mlp_names = ("w_gate", "w_up", "w_down", "ln2_g", "ln2_b")
    grid_spec = pltpu.PrefetchScalarGridSpec(
        num_scalar_prefetch=1,
        grid=(nb,),
        in_specs=[per_b((nrow, QCAT)), per_b((nq, d_f)), per_b((nq, KV_LORA)), per_b((nq, ROPE)),
                  per_b((nq, d_f)), per_b((nq, d_f)), per_b((nq, H_F))] + [hbm] * len(pools)
        + [row_block] + [_const_spec(wp[n].shape) for n in mlp_names],
        out_specs=(per_b((nrow, KV_LORA)), per_b((nq, d_f)), row_block),
        scratch_shapes=[pltpu.VMEM((2, pg) + arr.shape[1:], F32) for arr in pools] + [
            pltpu.SemaphoreType.DMA((2, len(pools))),
            pltpu.VMEM((2, pg * page, KV_LORA), BF16), pltpu.VMEM((2, ROPE, pg * page), BF16),
            pltpu.VMEM((2, d_f, pg * page), BF16), pltpu.VMEM((2, d_f, pg * page), BF16),
            pltpu.VMEM((2, ng + 1, nrow, LANES), F32), pltpu.VMEM((2, ng + 1, nrow, LANES), F32),
            pltpu.VMEM((ng + 1, nrow, KV_LORA), F32), pltpu.VMEM((ng + 1, nrow, d_f), F32),
            pltpu.VMEM((rows, d_ff), BF16), pltpu.VMEM((rows, D_MODEL), F32),
        ],
    )
    return pl.pallas_call(
        functools.partial(_decode_kernel, pg=pg, ng=ng, nq=nq, npages=npages),
        grid_spec=grid_spec,
        out_shape=(jax.ShapeDtypeStruct((nb, nrow, KV_LORA), BF16),
                   jax.ShapeDtypeStruct((nb, nq, d_f), BF16),
                   jax.ShapeDtypeStruct((t_p, D_MODEL), F32)),
        compiler_params=pltpu.CompilerParams(dimension_semantics=("arbitrary",)),
        name="decode_attn_prompt_mlp",
    )(pt_flat, qcat_s, fq_s, lat_s, krope_s, fk_s, fv_s, logf_s, *pools,
      h_prompt, *[wp[n] for n in mlp_names])


def _mem_decode_kernel(q_ref, k_ref, v_ref, o_ref):
    nseq, nrow, _ = q_ref.shape
    nkey = k_ref.shape[1]
    row = lax.broadcasted_iota(jnp.int32, (nrow, nkey), 0)
    col = lax.broadcasted_iota(jnp.int32, (nrow, nkey), 1)
    same_head = (col % H_M) == (row % H_M)
    for i in range(nseq):
        s = _dot_nt(q_ref[i], k_ref[i].astype(BF16)) * MEM_SCALE
        s = jnp.where(same_head, s, NEG_INF)
        m = jnp.max(s, axis=-1, keepdims=True)
        p = jnp.exp(s - m)
        l = jnp.sum(p, axis=-1, keepdims=True)
        o_ref[i] = (_dot(p.astype(BF16), v_ref[i].astype(BF16)) / l).astype(o_ref.dtype)


def _mem_decode(mq_s, mem_k, mem_v):
    nb, nrow, d = mq_s.shape
    nkey = mem_k.shape[1]
    nseq = min(MEM_DECODE_SEQS_PER_STEP, nb)
    return pl.pallas_call(
        _mem_decode_kernel,
        grid=(nb // nseq,),
        in_specs=[pl.BlockSpec((nseq, nrow, d), lambda b: (b, 0, 0)),
                  pl.BlockSpec((nseq, nkey, d), lambda b: (b, 0, 0)),
                  pl.BlockSpec((nseq, nkey, d), lambda b: (b, 0, 0))],
        out_specs=pl.BlockSpec((nseq, nrow, d), lambda b: (b, 0, 0)),
        out_shape=jax.ShapeDtypeStruct((nb, nrow, d), BF16),
        compiler_params=pltpu.CompilerParams(dimension_semantics=("parallel",)),
        name="mem_decode_attn",
    )(mq_s, mem_k, mem_v)


def _rope_tables(pos):
    half = ROPE // 2
    inv = jnp.exp(-math.log(ROPE_THETA) * jnp.arange(half, dtype=F32) / half)
    ang = pos.astype(F32)[:, None] * inv
    cos, sin = jnp.cos(ang), jnp.sin(ang)
    pad = jnp.zeros((pos.shape[0], LANES - ROPE), F32)
    return (jnp.concatenate([cos, cos, pad], axis=1), jnp.concatenate([-sin, sin, pad], axis=1))


def _prepare_weights(W_in, b_forget, b_gate, q_norm_g, W_uq, kv_norm_g, W_uk, W_uv,
                     W_br_a, W_br_f, W_br_m, W_out, ln1_g, ln1_b, w_gate, w_up, w_down, ln2_g, ln2_b):
    splits = (Q_LORA, KV_LORA, ROPE, H_F * HD_F, H_F * HD_F, H_F * HD_F, H_F, H_M * HD_M, 3 * D_MODEL)
    offs = [0]
    for n in splits:
        offs.append(offs[-1] + n)
    wq, wkv, wkr, wfq, wfk, wfv, wf, wmq, wg = (W_in[:, offs[i]:offs[i + 1]] for i in range(9))
    pad_to = lambda w: jnp.pad(w, ((0, 0), (0, LANES - w.shape[1])))
    wp = {}
    wp["w_front"] = jnp.concatenate([wq, wkv, wfq, wfk, wfv, wmq, pad_to(wkr), pad_to(wf)], axis=1).astype(BF16)
    w_nope = W_uq[:, :, :NOPE].reshape(Q_LORA, H_A * NOPE)
    w_rope = jnp.pad(W_uq[:, :, NOPE:], ((0, 0), (0, 0), (0, LANES - ROPE))).reshape(Q_LORA, H_A * LANES)
    wp["w_uq"] = jnp.concatenate([w_nope, w_rope], axis=1).astype(BF16)
    wp["w_uk"] = jnp.transpose(W_uk, (1, 2, 0)).astype(BF16)
    wp["w_uv"] = jnp.transpose(W_uv, (1, 0, 2)).astype(BF16)
    wp["q_g"] = q_norm_g.reshape(1, Q_LORA)
    wp["kv_g"] = kv_norm_g.reshape(1, KV_LORA)
    wp["b_f"] = jnp.pad(b_forget, (0, LANES - H_F)).reshape(1, LANES)
    wp["w_g"] = wg.astype(BF16)
    wp["b_g"] = b_gate.reshape(1, 3 * D_MODEL)
    wp["w_br_a"] = W_br_a.astype(BF16)
    wp["w_br_f"] = W_br_f.astype(BF16)
    wp["w_br_m"] = W_br_m.astype(BF16)
    wp["w_out"] = W_out.astype(BF16)
    wp["ln1_g"] = ln1_g.reshape(1, D_MODEL)
    wp["ln1_b"] = ln1_b.reshape(1, D_MODEL)
    wp["w_gate"] = w_gate.astype(BF16)
    wp["w_up"] = w_up.astype(BF16)
    wp["w_down"] = w_down.astype(BF16)
    wp["ln2_g"] = ln2_g.reshape(1, D_MODEL)
    wp["ln2_b"] = ln2_b.reshape(1, D_MODEL)
    return wp


def kernel(x_prompt, x_sample, cache_mla_latent, cache_mla_krope, cache_fox_k, cache_fox_v, cache_fox_logf, cache_mem_k, cache_mem_v, page_table, mem_prompt, W_in, b_forget, b_gate, q_norm_g, W_uq, kv_norm_g, W_uk, W_uv, W_mem_k, W_mem_v, W_br_a, W_br_f, W_br_m, W_out, ln1_g, ln1_b, w_gate, w_up, w_down, ln2_g, ln2_b):
    wp = _prepare_weights(W_in, b_forget, b_gate, q_norm_g, W_uq, kv_norm_g, W_uk, W_uv,
                          W_br_a, W_br_f, W_br_m, W_out, ln1_g, ln1_b, w_gate, w_up, w_down, ln2_g, ln2_b)
    bp, sp, _ = x_prompt.shape
    bs, ss, _ = x_sample.shape
    n_pool, page, _ = cache_mla_latent.shape
    n_mem = mem_prompt.shape[1]
    past_len = page_table.shape[1] * page

    xp = x_prompt.reshape(bp * sp, D_MODEL)
    cos_p, sin_p = _rope_tables(jnp.arange(sp, dtype=jnp.int32))
    (lat_p, krope_p, fk_p, fv_p, logf_p, qcat_p, kcat_p, fq_p, fk16_p, fv16_p, mq_p) = _front(xp, cos_p, sin_p, wp)
    w_mem = jnp.concatenate([W_mem_k.reshape(D_MODEL, H_M * HD_M),
                             W_mem_v.reshape(D_MODEL, H_M * HD_M)], axis=1).astype(BF16)
    mem_kv = _matmul(mem_prompt.reshape(bp * n_mem, D_MODEL), w_mem)
    mem_k_p = mem_kv[:, :H_M * HD_M].reshape(bp, n_mem, H_M * HD_M)
    mem_v_p = mem_kv[:, H_M * HD_M:].reshape(bp, n_mem, H_M * HD_M)

    logf_t = jnp.transpose(logf_p.reshape(bp, sp, H_F), (0, 2, 1))
    c = _cumsum_lanes(logf_t.reshape(bp * H_F, sp))
    c_row = c.reshape(bp, H_F // 2, 2, sp)
    c_col = jnp.broadcast_to(c.reshape(bp, H_F, sp, 1), (bp, H_F, sp, LANES))

    o_lat_p = _mla_flash(qcat_p.reshape(bp, sp, H_A * QCAT), kcat_p.reshape(bp, sp, QCAT))
    o_f_p = _fox_flash(fq_p.reshape(bp, sp, -1), fk16_p.reshape(bp, sp, -1), fv16_p.reshape(bp, sp, -1),
                       c_col, c_row)
    o_m_p = _mem_attn(mq_p.reshape(bp, sp, -1), mem_k_p, mem_v_p)
    h_p = _merge(xp, o_lat_p.reshape(bp * sp, -1), o_f_p.reshape(bp * sp, -1), o_m_p.reshape(bp * sp, -1), wp)

    xs = x_sample.reshape(bs * ss, D_MODEL)
    tm_s = min(ROW_TILE, bs * ss)
    pos_s = past_len + (jnp.arange(tm_s, dtype=jnp.int32) % ss)
    cos_s, sin_s = _rope_tables(pos_s)
    (lat_s, krope_s, fk_s, fv_s, logf_s, qcat_s, _, fq_s, _, _, mq_s) = _front(xs, cos_s, sin_s, wp)
    pool_lf_t = _pool_suffix(jnp.transpose(cache_fox_logf, (0, 2, 1)))
    pool_kr_t = jnp.transpose(cache_mla_krope, (0, 2, 1))
    pool_fk_t = jnp.transpose(cache_fox_k, (0, 2, 3, 1)).reshape(n_pool, H_F * HD_F, page)
    pool_fv_t = jnp.transpose(cache_fox_v, (0, 2, 3, 1)).reshape(n_pool, H_F * HD_F, page)
    o_lat_s, o_f_s, y_p = _decode_attn(
        page_table, qcat_s.reshape(bs, ss * H_A, QCAT), fq_s.reshape(bs, ss, -1),
        lat_s.reshape(bs, ss, -1), krope_s.reshape(bs, ss, -1), fk_s.reshape(bs, ss, -1),
        fv_s.reshape(bs, ss, -1), logf_s.reshape(bs, ss, -1),
        cache_mla_latent, pool_kr_t, pool_fk_t, pool_fv_t, pool_lf_t, h_p, wp)
    o_m_s = _mem_decode(mq_s.reshape(bs, ss * H_M, HD_M), cache_mem_k.reshape(bs, n_mem * H_M, HD_M),
                        cache_mem_v.reshape(bs, n_mem * H_M, HD_M))
    h_s = _merge(xs, o_lat_s.reshape(bs * ss, -1), o_f_s.reshape(bs * ss, -1), o_m_s.reshape(bs * ss, -1), wp)
    y_s = _mlp(h_s, wp)

    return (y_p.reshape(bp, sp, D_MODEL), y_s.reshape(bs, ss, D_MODEL),
            lat_p.reshape(bp, sp, KV_LORA), krope_p.reshape(bp, sp, ROPE),
            fk_p.reshape(bp, sp, H_F, HD_F), fv_p.reshape(bp, sp, H_F, HD_F), logf_p.reshape(bp, sp, H_F),
            mem_k_p.reshape(bp, n_mem, H_M, HD_M), mem_v_p.reshape(bp, n_mem, H_M, HD_M),
            lat_s.reshape(bs, ss, KV_LORA), krope_s.reshape(bs, ss, ROPE),
            fk_s.reshape(bs, ss, H_F, HD_F), fv_s.reshape(bs, ss, H_F, HD_F), logf_s.reshape(bs, ss, H_F))
```

```python
import functools
import math

import jax
import jax.numpy as jnp
from jax import lax
from jax.experimental import pallas as pl
from jax.experimental.pallas import tpu as pltpu

F32 = jnp.float32
BF16 = jnp.bfloat16

H_A, NOPE, ROPE, V_A = 8, 128, 64, 128
Q_LORA, KV_LORA = 512, 256
H_F, HD_F = 8, 64
H_M, HD_M = 4, 128
D_MODEL = 1024
ROPE_THETA = 10000.0
LN_EPS = 1e-5
RMS_EPS = 1e-6
DEPTH = 1
ALPHA = (2 * DEPTH) ** 0.25
MLA_SCALE = (NOPE + ROPE) ** -0.5
FOX_SCALE = HD_F ** -0.5
MEM_SCALE = HD_M ** -0.5

LANES = 128
QCAT = KV_LORA + LANES
NEG_INF = float("-inf")

ROW_TILE = 256
ATT_TILE = 512
SOFTMAX_ROWS = 32
MLA_HEADS_PER_STEP = 4
FOX_PAIRS_PER_STEP = 2
MEM_DECODE_SEQS_PER_STEP = 4
CUMSUM_CHUNK = 256
DECODE_PAGES_PER_GROUP = 8
POOL_PAGES_PER_STEP = 256
MLP_UP_PIECES = 5
MLP_DOWN_PIECES = 3


def _dot(a, b):
    return jnp.dot(a, b, preferred_element_type=F32)


def _dot_nt(a, b):
    return lax.dot_general(a, b, (((1,), (1,)), ((), ())), preferred_element_type=F32)


def _const_spec(shape):
    zeros = (0,) * len(shape)
    return pl.BlockSpec(shape, lambda *_: zeros, pipeline_mode=pl.Buffered(1))


def _split3(x):
    hi = x.astype(BF16).astype(F32)
    r = x - hi
    mid = r.astype(BF16).astype(F32)
    lo = (r - mid).astype(BF16).astype(F32)
    return hi, mid, lo


def _exact_dot_ones(x, tri):
    hi, mid, lo = _split3(x)
    return _dot(hi, tri) + _dot(mid, tri) + _dot(lo, tri)


def _layernorm(v, g, b):
    mu = jnp.mean(v, axis=-1, keepdims=True)
    d = v - mu
    var = jnp.mean(d * d, axis=-1, keepdims=True)
    return d * lax.rsqrt(var + LN_EPS) * g + b


def _rmsnorm(v, g):
    return v * lax.rsqrt(jnp.mean(v * v, axis=-1, keepdims=True) + RMS_EPS) * g


def _sigmoid(v):
    return 1.0 / (1.0 + jnp.exp(-v))


def _rope_tile(x, cos, sin_signed):
    lane = lax.broadcasted_iota(jnp.int32, x.shape, 1)
    rot = jnp.where(lane < ROPE // 2, pltpu.roll(x, LANES - ROPE // 2, 1), pltpu.roll(x, ROPE // 2, 1))
    return x * cos + rot * sin_signed


def _front_kernel(x_ref, cos_ref, sin_ref, w_ref, wuq_ref, wuk_ref, qg_ref, kvg_ref, bf_ref,
                  lat_ref, krope_ref, fk_ref, fv_ref, logf_ref,
                  qcat_ref, kcat_ref, fq_ref, fk16_ref, fv16_ref, mq_ref):
    xb = x_ref[...].astype(BF16)
    z = _dot(xb, w_ref[...])
    cos = cos_ref[...]
    sin = sin_ref[...]
    qn = _rmsnorm(z[:, 0:512], qg_ref[...]).astype(BF16)
    q = _dot(qn, wuq_ref[...])
    for h in range(H_A):
        q_nope = q[:, h * NOPE:(h + 1) * NOPE].astype(BF16)
        qcat_ref[:, h * QCAT:h * QCAT + KV_LORA] = _dot(q_nope, wuk_ref[h]).astype(BF16)
        xr = q[:, H_A * NOPE + h * LANES:H_A * NOPE + (h + 1) * LANES]
        qcat_ref[:, h * QCAT + KV_LORA:(h + 1) * QCAT] = _rope_tile(xr, cos, sin).astype(BF16)
    lat = _rmsnorm(z[:, 512:768], kvg_ref[...])
    lat_ref[...] = lat
    kcat_ref[:, 0:KV_LORA] = lat.astype(BF16)
    kr = _rope_tile(z[:, 2816:2944], cos, sin)
    krope_ref[...] = kr[:, 0:ROPE]
    kcat_ref[:, KV_LORA:QCAT] = kr.astype(BF16)
    fq_ref[...] = (z[:, 768:1280] * FOX_SCALE).astype(BF16)
    zfk = z[:, 1280:1792]
    zfv = z[:, 1792:2304]
    fk_ref[...] = zfk
    fv_ref[...] = zfv
    fk16_ref[...] = zfk.astype(BF16)
    fv16_ref[...] = zfv.astype(BF16)
    mq_ref[...] = z[:, 2304:2816].astype(BF16)
    zf = z[:, 2944:3072] + bf_ref[...]
    logf = jnp.minimum(zf, 0.0) - jnp.log1p(jnp.exp(-jnp.abs(zf)))
    logf_ref[...] = logf[:, 0:H_F]


def _front(x2d, cos_t, sin_t, wp):
    t = x2d.shape[0]
    tm = min(ROW_TILE, t)
    nblk = t // tm
    ntab = cos_t.shape[0] // tm
    row = lambda n: pl.BlockSpec((tm, n), lambda i: (i, 0))
    tab = pl.BlockSpec((tm, LANES), lambda i: (i % ntab, 0))
    out_shape = (
        jax.ShapeDtypeStruct((t, KV_LORA), F32), jax.ShapeDtypeStruct((t, ROPE), F32),
        jax.ShapeDtypeStruct((t, H_F * HD_F), F32), jax.ShapeDtypeStruct((t, H_F * HD_F), F32),
        jax.ShapeDtypeStruct((t, H_F), F32),
        jax.ShapeDtypeStruct((t, H_A * QCAT), BF16), jax.ShapeDtypeStruct((t, QCAT), BF16),
        jax.ShapeDtypeStruct((t, H_F * HD_F), BF16), jax.ShapeDtypeStruct((t, H_F * HD_F), BF16),
        jax.ShapeDtypeStruct((t, H_F * HD_F), BF16), jax.ShapeDtypeStruct((t, H_M * HD_M), BF16),
    )
    out_specs = (row(KV_LORA), row(ROPE), row(512), row(512), row(H_F),
                 row(H_A * QCAT), row(QCAT), row(512), row(512), row(512), row(512))
    return pl.pallas_call(
        _front_kernel,
        grid=(nblk,),
        in_specs=[row(D_MODEL), tab, tab,
                  _const_spec(wp["w_front"].shape), _const_spec(wp["w_uq"].shape),
                  _const_spec(wp["w_uk"].shape), _const_spec((1, Q_LORA)),
                  _const_spec((1, KV_LORA)), _const_spec((1, LANES))],
        out_specs=out_specs,
        out_shape=out_shape,
        compiler_params=pltpu.CompilerParams(dimension_semantics=("parallel",)),
        name="front",
    )(x2d, cos_t, sin_t, wp["w_front"], wp["w_uq"], wp["w_uk"], wp["q_g"], wp["kv_g"], wp["b_f"])


def _mm_kernel(x_ref, w_ref, o_ref):
    o_ref[...] = _dot(x_ref[...].astype(BF16), w_ref[...])


def _matmul(x2d, w):
    t, k = x2d.shape
    n = w.shape[1]
    tm = min(ROW_TILE, t)
    return pl.pallas_call(
        _mm_kernel,
        grid=(t // tm,),
        in_specs=[pl.BlockSpec((tm, k), lambda i: (i, 0)), _const_spec((k, n))],
        out_specs=pl.BlockSpec((tm, n), lambda i: (i, 0)),
        out_shape=jax.ShapeDtypeStruct((t, n), F32),
        compiler_params=pltpu.CompilerParams(dimension_semantics=("parallel",)),
        name="mem_kv_proj",
    )(x2d, w)


def _cumsum_kernel(x_ref, o_ref):
    n = x_ref.shape[1]
    ck = min(CUMSUM_CHUNK, n)
    r = lax.broadcasted_iota(jnp.int32, (ck, ck), 0)
    c = lax.broadcasted_iota(jnp.int32, (ck, ck), 1)
    tri = (r <= c).astype(F32)
    carry = jnp.zeros((x_ref.shape[0], 1), F32)
    for j in range(n // ck):
        cs = _exact_dot_ones(x_ref[:, j * ck:(j + 1) * ck], tri) + carry
        o_ref[:, j * ck:(j + 1) * ck] = cs
        carry = cs[:, ck - 1:ck]


def _cumsum_lanes(x2d):
    return pl.pallas_call(
        _cumsum_kernel,
        out_shape=jax.ShapeDtypeStruct(x2d.shape, F32),
        name="logf_cumsum",
    )(x2d)


def _lane_tile(x, width):
    n = width // LANES
    return x if n == 1 else jnp.concatenate([x] * n, axis=1)


def _softmax_rows(s_ref, p_ref, m_ref, l_ref, a_ref, idx, logit_fn, diagonal):
    tq, tk = s_ref.shape[1:]
    assert not diagonal or tq == tk
    rc = min(SOFTMAX_ROWS, tq)
    for c in range(tq // rc):
        rows = slice(c * rc, (c + 1) * rc)
        w = min(tk, pl.cdiv((c + 1) * rc, LANES) * LANES) if diagonal else tk
        s = logit_fn(s_ref[idx, rows, 0:w], rows, w)
        if diagonal:
            qpos = c * rc + lax.broadcasted_iota(jnp.int32, (rc, w), 0)
            kpos = lax.broadcasted_iota(jnp.int32, (rc, w), 1)
            s = jnp.where(kpos <= qpos, s, NEG_INF)
        m_prev = m_ref[idx, rows, :]
        m_new = jnp.maximum(m_prev, jnp.max(s, axis=-1, keepdims=True))
        alpha = jnp.exp(m_prev - m_new)
        p = jnp.exp(s - _lane_tile(m_new, w))
        l_ref[idx, rows, :] = alpha * l_ref[idx, rows, :] + jnp.sum(p, axis=-1, keepdims=True)
        m_ref[idx, rows, :] = m_new
        a_ref[idx, rows, :] = alpha
        p_ref[idx, rows, 0:w] = p.astype(BF16)
        if w < tk:
            p_ref[idx, rows, w:tk] = jnp.zeros((rc, tk - w), BF16)


def _flash_init(ki, m_ref, l_ref, acc_ref):
    @pl.when(ki == 0)
    def _():
        m_ref[...] = jnp.full(m_ref.shape, NEG_INF, F32)
        l_ref[...] = jnp.zeros(l_ref.shape, F32)
        acc_ref[...] = jnp.zeros(acc_ref.shape, F32)


def _causal_steps(n):
    pairs = [(qi, ki) for qi in range(n) for ki in range(qi + 1)]
    return (jnp.array([p[0] for p in pairs], jnp.int32), jnp.array([p[1] for p in pairs], jnp.int32))


def _flash_scratch(nh, t, width):
    return [pltpu.VMEM((nh, t, t), F32), pltpu.VMEM((nh, t, t), BF16),
            pltpu.VMEM((nh, t, LANES), F32), pltpu.VMEM((nh, t, LANES), F32),
            pltpu.VMEM((nh, t, LANES), F32), pltpu.VMEM((nh, t, width), F32)]


def _mla_flash_kernel(qt_ref, kt_ref, q_ref, k_ref, o_ref, s_ref, p_ref, m_ref, l_ref, a_ref, acc_ref):
    step_id = pl.program_id(2)
    qi = qt_ref[step_id]
    ki = kt_ref[step_id]
    tq = q_ref.shape[1]
    tk = k_ref.shape[1]
    nh = s_ref.shape[0]
    _flash_init(ki, m_ref, l_ref, acc_ref)

    def step(masked):
        k = k_ref[0]
        for g in range(nh):
            s_ref[g] = _dot_nt(q_ref[0, :, g * QCAT:(g + 1) * QCAT], k)
        for g in range(nh):
            _softmax_rows(s_ref, p_ref, m_ref, l_ref, a_ref, g, lambda s, rows, w: s * MLA_SCALE, masked)
            acc_ref[g] = _lane_tile(a_ref[g], KV_LORA) * acc_ref[g] + _dot(p_ref[g], k[:, 0:KV_LORA])

    @pl.when(ki < qi)
    def _():
        step(False)

    @pl.when(ki == qi)
    def _():
        step(True)
        for g in range(nh):
            o_ref[0, :, g * KV_LORA:(g + 1) * KV_LORA] = (
                acc_ref[g] / _lane_tile(l_ref[g], KV_LORA)).astype(o_ref.dtype)


def _mla_flash(qcat, kcat):
    b, s, _ = qcat.shape
    t = min(ATT_TILE, s)
    qt, kt = _causal_steps(s // t)
    hg = MLA_HEADS_PER_STEP
    grid_spec = pltpu.PrefetchScalarGridSpec(
        num_scalar_prefetch=2,
        grid=(b, H_A // hg, qt.shape[0]),
        in_specs=[pl.BlockSpec((1, t, hg * QCAT), lambda b, h, i, qt, kt: (b, qt[i], h)),
                  pl.BlockSpec((1, t, QCAT), lambda b, h, i, qt, kt: (b, kt[i], 0))],
        out_specs=pl.BlockSpec((1, t, hg * KV_LORA), lambda b, h, i, qt, kt: (b, qt[i], h)),
        scratch_shapes=_flash_scratch(hg, t, KV_LORA),
    )
    return pl.pallas_call(
        _mla_flash_kernel,
        grid_spec=grid_spec,
        out_shape=jax.ShapeDtypeStruct((b, s, H_A * KV_LORA), BF16),
        compiler_params=pltpu.CompilerParams(dimension_semantics=("parallel", "parallel", "arbitrary")),
        name="mla_prompt_attn",
    )(qt, kt, qcat, kcat)


def _fox_flash_kernel(qt_ref, kt_ref, q_ref, k_ref, v_ref, cq_ref, ck_ref, o_ref,
                      s_ref, p_ref, m_ref, l_ref, a_ref, acc_ref):
    step_id = pl.program_id(2)
    qi = qt_ref[step_id]
    ki = kt_ref[step_id]
    tq = q_ref.shape[1]
    tk = k_ref.shape[1]
    npair = s_ref.shape[0] // 2
    _flash_init(ki, m_ref, l_ref, acc_ref)

    def step(masked):
        lane = lax.broadcasted_iota(jnp.int32, (tq, LANES), 1)
        for pr in range(npair):
            q2 = q_ref[0, :, pr * LANES:(pr + 1) * LANES]
            k2 = k_ref[0, :, pr * LANES:(pr + 1) * LANES]
            for hh in range(2):
                keep = (lane >= HD_F) if hh else (lane < HD_F)
                s_ref[2 * pr + hh] = _dot_nt(jnp.where(keep, q2, jnp.zeros_like(q2)), k2)
        for pr in range(npair):
            v2 = v_ref[0, :, pr * LANES:(pr + 1) * LANES]
            for hh in range(2):
                h = 2 * pr + hh

                def logits(s, rows, w, h=h, pr=pr, hh=hh):
                    return s + _lane_tile(cq_ref[0, h, rows, :], w) - ck_ref[0, pr, hh:hh + 1, 0:w]
                _softmax_rows(s_ref, p_ref, m_ref, l_ref, a_ref, h, logits, masked)
                acc_ref[h] = a_ref[h] * acc_ref[h] + _dot(p_ref[h], v2)

    @pl.when(ki < qi)
    def _():
        step(False)

    @pl.when(ki == qi)
    def _():
        step(True)
        lane = lax.broadcasted_iota(jnp.int32, (tq, LANES), 1)
        for pr in range(npair):
            o0 = acc_ref[2 * pr] / l_ref[2 * pr]
            o1 = acc_ref[2 * pr + 1] / l_ref[2 * pr + 1]
            o_ref[0, :, pr * LANES:(pr + 1) * LANES] = jnp.where(lane < HD_F, o0, o1).astype(o_ref.dtype)


def _fox_flash(fq, fk16, fv16, c_col, c_row):
    b, s, _ = fq.shape
    t = min(ATT_TILE, s)
    qt, kt = _causal_steps(s // t)
    pp = FOX_PAIRS_PER_STEP
    qspec = pl.BlockSpec((1, t, pp * LANES), lambda b, h, i, qt, kt: (b, qt[i], h))
    kspec = pl.BlockSpec((1, t, pp * LANES), lambda b, h, i, qt, kt: (b, kt[i], h))
    grid_spec = pltpu.PrefetchScalarGridSpec(
        num_scalar_prefetch=2,
        grid=(b, H_F // (2 * pp), qt.shape[0]),
        in_specs=[qspec, kspec, kspec,
                  pl.BlockSpec((1, 2 * pp, t, LANES), lambda b, h, i, qt, kt: (b, h, qt[i], 0)),
                  pl.BlockSpec((1, pp, 2, t), lambda b, h, i, qt, kt: (b, h, 0, kt[i]))],
        out_specs=qspec,
        scratch_shapes=_flash_scratch(2 * pp, t, LANES),
    )
    return pl.pallas_call(
        _fox_flash_kernel,
        grid_spec=grid_spec,
        out_shape=jax.ShapeDtypeStruct((b, s, H_F * HD_F), BF16),
        compiler_params=pltpu.CompilerParams(dimension_semantics=("parallel", "parallel", "arbitrary")),
        name="fox_prompt_attn",
    )(qt, kt, fq, fk16, fv16, c_col, c_row)


def _mem_attn_kernel(q_ref, k_ref, v_ref, o_ref, s_ref, p_ref, l_ref):
    tq = q_ref.shape[1]
    rc = min(SOFTMAX_ROWS, tq)
    heads = [slice(h * HD_M, (h + 1) * HD_M) for h in range(H_M)]
    for h, hs in enumerate(heads):
        s_ref[h] = _dot_nt(q_ref[0, :, hs], k_ref[0, :, hs].astype(BF16))
    for h, hs in enumerate(heads):
        for c in range(tq // rc):
            rows = slice(c * rc, (c + 1) * rc)
            s = s_ref[h, rows, :] * MEM_SCALE
            p = jnp.exp(s - jnp.max(s, axis=-1, keepdims=True))
            l_ref[h, rows, :] = jnp.broadcast_to(jnp.sum(p, axis=-1, keepdims=True), (rc, HD_M))
            p_ref[h, rows, :] = p.astype(BF16)
        o_ref[0, :, hs] = (_dot(p_ref[h], v_ref[0, :, hs].astype(BF16)) / l_ref[h]).astype(o_ref.dtype)


def _mem_attn(mq, mem_k, mem_v):
    b, s, d = mq.shape
    nm = mem_k.shape[1]
    t = min(ATT_TILE, s)
    return pl.pallas_call(
        _mem_attn_kernel,
        grid=(b, s // t),
        in_specs=[pl.BlockSpec((1, t, d), lambda b, qi: (b, qi, 0)),
                  pl.BlockSpec((1, nm, d), lambda b, qi: (b, 0, 0)),
                  pl.BlockSpec((1, nm, d), lambda b, qi: (b, 0, 0))],
        out_specs=pl.BlockSpec((1, t, d), lambda b, qi: (b, qi, 0)),
        out_shape=jax.ShapeDtypeStruct((b, s, d), BF16),
        scratch_shapes=[pltpu.VMEM((H_M, t, nm), F32), pltpu.VMEM((H_M, t, nm), BF16),
                        pltpu.VMEM((H_M, t, HD_M), F32)],
        compiler_params=pltpu.CompilerParams(dimension_semantics=("parallel", "parallel")),
        name="mem_prompt_attn",
    )(mq, mem_k, mem_v)


def _merge_kernel(x_ref, olat_ref, of_ref, om_ref, wg_ref, bg_ref, wuv_ref, wa_ref, wf_ref, wm_ref,
                  wo_ref, g_ref, b_ref, h_ref, oa_ref):
    x = x_ref[...]
    xb = x.astype(BF16)
    for h in range(H_A):
        oa_ref[:, h * V_A:(h + 1) * V_A] = _dot(
            olat_ref[:, h * KV_LORA:(h + 1) * KV_LORA], wuv_ref[h]).astype(BF16)
    branches = (_dot(oa_ref[...], wa_ref[...]), _dot(of_ref[...], wf_ref[...]),
                _dot(om_ref[...], wm_ref[...]))
    merged = None
    for i, br in enumerate(branches):
        gate = _sigmoid(_dot(xb, wg_ref[:, i * D_MODEL:(i + 1) * D_MODEL])
                        + bg_ref[:, i * D_MODEL:(i + 1) * D_MODEL])
        merged = gate * br if merged is None else merged + gate * br
    pre = ALPHA * x + _dot(merged.astype(BF16), wo_ref[...])
    h_ref[...] = _layernorm(pre, g_ref[...], b_ref[...])


def _merge(x2d, o_lat, o_f, o_m, wp):
    t = x2d.shape[0]
    tm = min(ROW_TILE, t)
    row = lambda n: pl.BlockSpec((tm, n), lambda i: (i, 0))
    names = ("w_g", "b_g", "w_uv", "w_br_a", "w_br_f", "w_br_m", "w_out", "ln1_g", "ln1_b")
    return pl.pallas_call(
        _merge_kernel,
        grid=(t // tm,),
        in_specs=[row(D_MODEL), row(H_A * KV_LORA), row(H_F * HD_F), row(H_M * HD_M)]
        + [_const_spec(wp[n].shape) for n in names],
        out_specs=row(D_MODEL),
        out_shape=jax.ShapeDtypeStruct((t, D_MODEL), F32),
        scratch_shapes=[pltpu.VMEM((tm, H_A * V_A), BF16)],
        compiler_params=pltpu.CompilerParams(dimension_semantics=("parallel",)),
        name="merge_ln1",
    )(x2d, o_lat, o_f, o_m, *[wp[n] for n in names])


def _mlp_kernel(h_ref, wg_ref, wu_ref, wd_ref, g_ref, b_ref, y_ref, a_ref, *, n_chunks):
    h = h_ref[...]
    hb = h.astype(BF16)
    fc = wg_ref.shape[1] // n_chunks
    for c in range(n_chunks):
        gt = _dot(hb, wg_ref[:, c * fc:(c + 1) * fc])
        up = _dot(hb, wu_ref[:, c * fc:(c + 1) * fc])
        a_ref[:, c * fc:(c + 1) * fc] = (gt * _sigmoid(gt) * up).astype(BF16)
    y_ref[...] = _layernorm(ALPHA * h + _dot(a_ref[...], wd_ref[...]), g_ref[...], b_ref[...])


def _lane_tile_chunks(width, parts):
    tiles = width // LANES
    bounds = [LANES * ((i * tiles) // parts) for i in range(parts + 1)]
    return [(bounds[i], bounds[i + 1]) for i in range(parts) if bounds[i + 1] > bounds[i]]


def _mlp_work_items(h_ref, wg_ref, wu_ref, wd_ref, g_ref, b_ref, y_ref, a_ref, pre_ref):
    def up_piece(c0, c1):
        hb = h_ref[...].astype(BF16)
        gt = _dot(hb, wg_ref[:, c0:c1])
        up = _dot(hb, wu_ref[:, c0:c1])
        a_ref[:, c0:c1] = (gt * _sigmoid(gt) * up).astype(BF16)

    def down_piece(c0, c1, last):
        pre_ref[:, c0:c1] = _dot(a_ref[...], wd_ref[:, c0:c1])
        if last:
            y_ref[...] = _layernorm(ALPHA * h_ref[...] + pre_ref[...], g_ref[...], b_ref[...])

    items = [functools.partial(up_piece, c0, c1)
             for c0, c1 in _lane_tile_chunks(wg_ref.shape[1], MLP_UP_PIECES)]
    down = _lane_tile_chunks(wd_ref.shape[1], MLP_DOWN_PIECES)
    items += [functools.partial(down_piece, c0, c1, i == len(down) - 1) for i, (c0, c1) in enumerate(down)]
    return items


def _mlp(h2d, wp):
    t = h2d.shape[0]
    tm = min(ROW_TILE, t)
    d_ff = wp["w_gate"].shape[1]
    n_chunks = 2
    row = lambda n: pl.BlockSpec((tm, n), lambda i: (i, 0))
    names = ("w_gate", "w_up", "w_down", "ln2_g", "ln2_b")
    return pl.pallas_call(
        functools.partial(_mlp_kernel, n_chunks=n_chunks),
        grid=(t // tm,),
        in_specs=[row(D_MODEL)] + [_const_spec(wp[n].shape) for n in names],
        out_specs=row(D_MODEL),
        out_shape=jax.ShapeDtypeStruct((t, D_MODEL), F32),
        scratch_shapes=[pltpu.VMEM((tm, d_ff), BF16)],
        compiler_params=pltpu.CompilerParams(dimension_semantics=("parallel",)),
        name="mlp_ln2",
    )(h2d, *[wp[n] for n in names])


def _rows_per_query(x, nq):
    return jnp.concatenate([jnp.broadcast_to(x[q:q + 1], (8, x.shape[1])) for q in range(nq)], axis=0)


def _page_copies(pt_ref, pools, bufs, sem, b, group, slot, pg, npages):
    copies = []
    for i in range(pg):
        pid = pt_ref[b * npages + group * pg + i]
        for a, (pool, buf) in enumerate(zip(pools, bufs)):
            copy = pltpu.make_async_copy(pool.at[pid], buf.at[slot, i], sem.at[slot, a])
            copies.append((copy, i % 2))
    return copies


def _pool_suffix_kernel(x_ref, o_ref):
    pb = x_ref.shape[0]
    l = x_ref[...].reshape(pb * H_F, LANES)
    r = lax.broadcasted_iota(jnp.int32, (LANES, LANES), 0)
    c = lax.broadcasted_iota(jnp.int32, (LANES, LANES), 1)
    excl = _exact_dot_ones(l, (r > c).astype(F32))
    tot = jnp.broadcast_to(excl[:, 0:1] + l[:, 0:1], excl.shape)
    o_ref[:, 0:H_F, :] = excl.reshape(pb, H_F, LANES)
    o_ref[:, H_F:2 * H_F, :] = tot.reshape(pb, H_F, LANES)


def _pool_suffix(logf_t):
    n_pool = logf_t.shape[0]
    pb = min(POOL_PAGES_PER_STEP, n_pool)
    return pl.pallas_call(
        _pool_suffix_kernel,
        grid=(n_pool // pb,),
        in_specs=[pl.BlockSpec((pb, H_F, LANES), lambda i: (i, 0, 0))],
        out_specs=pl.BlockSpec((pb, 2 * H_F, LANES), lambda i: (i, 0, 0)),
        out_shape=jax.ShapeDtypeStruct((n_pool, 2 * H_F, LANES), F32),
        compiler_params=pltpu.CompilerParams(dimension_semantics=("parallel",)),
        name="pool_logf_suffix",
    )(logf_t)


def _decode_kernel(pt_ref, qcat_ref, fq_ref, latn_ref, krn_ref, fkn_ref, fvn_ref, lfn_ref,
                   lat_hbm, kr_hbm, fk_hbm, fv_hbm, lf_hbm,
                   h_ref, wg_ref, wu_ref, wd_ref, g2_ref, b2_ref,
                   olat_ref, of_ref, y_ref,
                   lat_buf, kr_buf, fk_buf, fv_buf, lf_buf, sem,
                   pm_ref, pl_ref, pacca_ref, paccf_ref, a_ref, pre_ref, *, pg, ng, nq, npages):
    mlp_items = _mlp_work_items(h_ref, wg_ref, wu_ref, wd_ref, g2_ref, b2_ref, y_ref, a_ref, pre_ref)
    b = pl.program_id(0)
    nb = pl.num_programs(0)
    nrow = nq * 8
    page = lat_buf.shape[2]
    pools = (lat_hbm, kr_hbm, fk_hbm, fv_hbm, lf_hbm)
    bufs = (lat_buf, kr_buf, fk_buf, fv_buf, lf_buf)
    copies = functools.partial(_page_copies, pt_ref, pools, bufs, sem, pg=pg, npages=npages)

    @pl.when(b == 0)
    def _():
        for c, prio in copies(b, ng - 1, 0):
            c.start(priority=prio)

    row512 = lax.broadcasted_iota(jnp.int32, (nrow, H_F * HD_F), 0)
    lane512 = lax.broadcasted_iota(jnp.int32, (nrow, H_F * HD_F), 1)
    head_mask = (lane512 // HD_F) == (row512 % 8)

    def store_partial(idx, k, m, l, acc, acc_ref):
        pm_ref[idx, k] = jnp.broadcast_to(m, (nrow, LANES))
        pl_ref[idx, k] = jnp.broadcast_to(l, (nrow, LANES))
        acc_ref[k] = acc

    qrow = lax.broadcasted_iota(jnp.int32, (nrow, 1), 0) // 8
    qbd = jnp.where(head_mask, _rows_per_query(fq_ref[0].astype(F32), nq), 0.0)
    qbd16 = qbd.astype(BF16)
    lfn = lfn_ref[0]
    r8 = lax.broadcasted_iota(jnp.int32, (nrow, H_F), 0)
    l8 = lax.broadcasted_iota(jnp.int32, (nrow, H_F), 1)
    pick = l8 == (r8 % 8)
    cum = []
    for q in range(nq):
        cum.append(lfn[q:q + 1] if q == 0 else cum[-1] + lfn[q:q + 1])
    cq = jnp.sum(jnp.where(pick, jnp.concatenate(
        [jnp.broadcast_to(c, (8, H_F)) for c in cum], axis=0), 0.0), axis=1, keepdims=True)
    fkn = fkn_ref[0]
    fvn = fvn_ref[0]
    sf = []
    for s in range(nq):
        ck = jnp.sum(jnp.where(pick, jnp.broadcast_to(cum[s], (nrow, H_F)), 0.0),
                     axis=1, keepdims=True)
        v = jnp.sum(qbd * fkn[s:s + 1], axis=1, keepdims=True) + cq - ck
        sf.append(jnp.where(qrow >= s, v, NEG_INF))
    qc16 = qcat_ref[0]
    qc = qc16.astype(F32)
    latn = latn_ref[0]
    krn = krn_ref[0]
    sa = []
    for s in range(nq):
        v = (jnp.sum(qc[:, 0:KV_LORA] * latn[s:s + 1], axis=1, keepdims=True)
             + jnp.sum(qc[:, KV_LORA:KV_LORA + ROPE] * krn[s:s + 1], axis=1, keepdims=True))
        sa.append(jnp.where(qrow >= s, v * MLA_SCALE, NEG_INF))
    for idx, (sc, vals, acc_ref) in enumerate(((sa, latn, pacca_ref), (sf, fvn, paccf_ref))):
        m = sc[0]
        for s in range(1, nq):
            m = jnp.maximum(m, sc[s])
        l = jnp.zeros_like(m)
        acc = jnp.zeros(acc_ref.shape[1:], F32)
        for s in range(nq):
            p = jnp.exp(sc[s] - m)
            l = l + p
            acc = acc + p * vals[s:s + 1]
        store_partial(idx, ng, m, l, acc, acc_ref)

    carry = jnp.zeros((H_F, LANES), F32)
    for k in range(ng):
        slot = k % 2
        group = ng - 1 - k
        if k + 1 < ng:
            for c, prio in copies(b, group - 1, 1 - slot):
                c.start(priority=prio)
        else:
            @pl.when(b + 1 < nb)
            def _():
                for c, prio in copies(b + 1, ng - 1, 1 - slot):
                    c.start(priority=prio)
        for j, item in enumerate(mlp_items):
            if (j * ng) // len(mlp_items) == k:
                item()
        for c, _ in copies(b, group, slot):
            c.wait()

        pieces = [None] * pg
        for i in reversed(range(pg)):
            sums = lf_buf[slot, i]
            pieces[i] = sums[0:H_F] + carry
            carry = carry + sums[H_F:2 * H_F]
        bias = jnp.concatenate(pieces, axis=1)
        bias = jnp.concatenate([bias] * nq, axis=0)

        lat16 = [lat_buf[slot, i].astype(BF16) for i in range(pg)]
        sf = jnp.concatenate([_dot(qbd16, fk_buf[slot, i].astype(BF16)) for i in range(pg)], axis=1)
        sa = jnp.concatenate(
            [_dot_nt(qc16[:, 0:KV_LORA], lat16[i])
             + _dot(qc16[:, KV_LORA:KV_LORA + ROPE], kr_buf[slot, i].astype(BF16)) for i in range(pg)], axis=1)

        sf = sf + bias + cq
        m = jnp.max(sf, axis=-1, keepdims=True)
        p = jnp.exp(sf - m)
        p16 = p.astype(BF16)
        acc = _dot_nt(p16[:, 0:page], fv_buf[slot, 0].astype(BF16))
        for i in range(1, pg):
            acc = acc + _dot_nt(p16[:, i * page:(i + 1) * page], fv_buf[slot, i].astype(BF16))
        store_partial(1, k, m, jnp.sum(p, axis=-1, keepdims=True), acc, paccf_ref)

        sa = sa * MLA_SCALE
        m = jnp.max(sa, axis=-1, keepdims=True)
        p = jnp.exp(sa - m)
        p16 = p.astype(BF16)
        acc = _dot(p16[:, 0:page], lat16[0])
        for i in range(1, pg):
            acc = acc + _dot(p16[:, i * page:(i + 1) * page], lat16[i])
        store_partial(0, k, m, jnp.sum(p, axis=-1, keepdims=True), acc, pacca_ref)

    outs = []
    for idx, acc_ref in enumerate((pacca_ref, paccf_ref)):
        width = acc_ref.shape[2]
        m = pm_ref[idx, 0]
        for k in range(1, ng + 1):
            m = jnp.maximum(m, pm_ref[idx, k])
        l = jnp.zeros((nrow, LANES), F32)
        acc = jnp.zeros((nrow, width), F32)
        for k in range(ng + 1):
            w = jnp.exp(pm_ref[idx, k] - m)
            l = l + w * pl_ref[idx, k]
            acc = acc + _lane_tile(w, width) * acc_ref[k]
        outs.append(acc / _lane_tile(l, width))
    olat_ref[0] = outs[0].astype(olat_ref.dtype)
    of = jnp.where(head_mask, outs[1], 0.0)
    of_ref[0] = jnp.sum(of.reshape(nq, 8, H_F * HD_F), axis=1).astype(of_ref.dtype)


def _decode_attn(page_table, qcat_s, fq_s, lat_s, krope_s, fk_s, fv_s, logf_s,
                 pool_lat, pool_kr, pool_fk, pool_fv, pool_lf, h_prompt, wp):
    nb, npages = page_table.shape
    nq = fq_s.shape[1]
    page = pool_lat.shape[1]
    pg = min(DECODE_PAGES_PER_GROUP, npages)
    ng = npages // pg
    assert ng * pg == npages and ng % 2 == 0, "page groups must alternate between the two buffer slots"
    nrow = nq * 8
    d_f = H_F * HD_F
    d_ff = wp["w_gate"].shape[1]
    t_p = h_prompt.shape[0]
    rows = t_p // nb
    assert rows * nb == t_p and rows % 8 == 0
    pt_flat = page_table.reshape(-1)

    def per_b(shape):
        return pl.BlockSpec((1,) + shape, lambda b, pt: (b,) + (0,) * len(shape))

    hbm = pl.BlockSpec(memory_space=pl.ANY)
    row_block = pl.BlockSpec((rows, D_MODEL), lambda b, pt: (b, 0))
    pools = (pool_lat, pool_kr, pool_fk, pool_fv, pool_lf)
    mlp_names = ("w_gate", "w_up", "w_down", "ln2_g", "ln2_b")
    grid_spec = pltpu.PrefetchScalarGridSpec(
        num_scalar_prefetch=1,
        grid=(nb,),
        in_specs=[per_b((nrow, QCAT)), per_b((nq, d_f)), per_b((nq, KV_LORA)), per_b((nq, ROPE)),
                  per_b((nq, d_f)), per_b((nq, d_f)), per_b((nq, H_F))] + [hbm] * len(pools)
        + [row_block] + [_const_spec(wp[n].shape) for n in mlp_names],
        out_specs=(per_b((nrow, KV_LORA)), per_b((nq, d_f)), row_block),
        scratch_shapes=[pltpu.VMEM((2, pg) + arr.shape[1:], F32) for arr in pools] + [
            pltpu.SemaphoreType.DMA((2, len(pools))),
            pltpu.VMEM((2, ng + 1, nrow, LANES), F32), pltpu.VMEM((2, ng + 1, nrow, LANES), F32),
            pltpu.VMEM((ng + 1, nrow, KV_LORA), F32), pltpu.VMEM((ng + 1, nrow, d_f), F32),
            pltpu.VMEM((rows, d_ff), BF16), pltpu.VMEM((rows, D_MODEL), F32),
        ],
    )
    return pl.pallas_call(
        functools.partial(_decode_kernel, pg=pg, ng=ng, nq=nq, npages=npages),
        grid_spec=grid_spec,
        out_shape=(jax.ShapeDtypeStruct((nb, nrow, KV_LORA), BF16),
                   jax.ShapeDtypeStruct((nb, nq, d_f), BF16),
                   jax.ShapeDtypeStruct((t_p, D_MODEL), F32)),
        compiler_params=pltpu.CompilerParams(dimension_semantics=("arbitrary",)),
        name="decode_attn_prompt_mlp",
    )(pt_flat, qcat_s, fq_s, lat_s, krope_s, fk_s, fv_s, logf_s, *pools,
      h_prompt, *[wp[n] for n in mlp_names])


def _mem_decode_kernel(q_ref, k_ref, v_ref, o_ref):
    nseq, nrow, _ = q_ref.shape
    nkey = k_ref.shape[1]
    row = lax.broadcasted_iota(jnp.int32, (nrow, nkey), 0)
    col = lax.broadcasted_iota(jnp.int32, (nrow, nkey), 1)
    same_head = (col % H_M) == (row % H_M)
    for i in range(nseq):
        s = _dot_nt(q_ref[i], k_ref[i].astype(BF16)) * MEM_SCALE
        s = jnp.where(same_head, s, NEG_INF)
        m = jnp.max(s, axis=-1, keepdims=True)
        p = jnp.exp(s - m)
        l = jnp.sum(p, axis=-1, keepdims=True)
        o_ref[i] = (_dot(p.astype(BF16), v_ref[i].astype(BF16)) / l).astype(o_ref.dtype)


def _mem_decode(mq_s, mem_k, mem_v):
    nb, nrow, d = mq_s.shape
    nkey = mem_k.shape[1]
    nseq = min(MEM_DECODE_SEQS_PER_STEP, nb)
    return pl.pallas_call(
        _mem_decode_kernel,
        grid=(nb // nseq,),
        in_specs=[pl.BlockSpec((nseq, nrow, d), lambda b: (b, 0, 0)),
                  pl.BlockSpec((nseq, nkey, d), lambda b: (b, 0, 0)),
                  pl.BlockSpec((nseq, nkey, d), lambda b: (b, 0, 0))],
        out_specs=pl.BlockSpec((nseq, nrow, d), lambda b: (b, 0, 0)),
        out_shape=jax.ShapeDtypeStruct((nb, nrow, d), BF16),
        compiler_params=pltpu.CompilerParams(dimension_semantics=("parallel",)),
        name="mem_decode_attn",
    )(mq_s, mem_k, mem_v)


def _rope_tables(pos):
    half = ROPE // 2
    inv = jnp.exp(-math.log(ROPE_THETA) * jnp.arange(half, dtype=F32) / half)
    ang = pos.astype(F32)[:, None] * inv
    cos, sin = jnp.cos(ang), jnp.sin(ang)
    pad = jnp.zeros((pos.shape[0], LANES - ROPE), F32)
    return (jnp.concatenate([cos, cos, pad], axis=1), jnp.concatenate([-sin, sin, pad], axis=1))


def _prepare_weights(W_in, b_forget, b_gate, q_norm_g, W_uq, kv_norm_g, W_uk, W_uv,
                     W_br_a, W_br_f, W_br_m, W_out, ln1_g, ln1_b, w_gate, w_up, w_down, ln2_g, ln2_b):
    splits = (Q_LORA, KV_LORA, ROPE, H_F * HD_F, H_F * HD_F, H_F * HD_F, H_F, H_M * HD_M, 3 * D_MODEL)
    offs = [0]
    for n in splits:
        offs.append(offs[-1] + n)
    wq, wkv, wkr, wfq, wfk, wfv, wf, wmq, wg = (W_in[:, offs[i]:offs[i + 1]] for i in range(9))
    pad_to = lambda w: jnp.pad(w, ((0, 0), (0, LANES - w.shape[1])))
    wp = {}
    wp["w_front"] = jnp.concatenate([wq, wkv, wfq, wfk, wfv, wmq, pad_to(wkr), pad_to(wf)], axis=1).astype(BF16)
    w_nope = W_uq[:, :, :NOPE].reshape(Q_LORA, H_A * NOPE)
    w_rope = jnp.pad(W_uq[:, :, NOPE:], ((0, 0), (0, 0), (0, LANES - ROPE))).reshape(Q_LORA, H_A * LANES)
    wp["w_uq"] = jnp.concatenate([w_nope, w_rope], axis=1).astype(BF16)
    wp["w_uk"] = jnp.transpose(W_uk, (1, 2, 0)).astype(BF16)
    wp["w_uv"] = jnp.transpose(W_uv, (1, 0, 2)).astype(BF16)
    wp["q_g"] = q_norm_g.reshape(1, Q_LORA)
    wp["kv_g"] = kv_norm_g.reshape(1, KV_LORA)
    wp["b_f"] = jnp.pad(b_forget, (0, LANES - H_F)).reshape(1, LANES)
    wp["w_g"] = wg.astype(BF16)
    wp["b_g"] = b_gate.reshape(1, 3 * D_MODEL)
    wp["w_br_a"] = W_br_a.astype(BF16)
    wp["w_br_f"] = W_br_f.astype(BF16)
    wp["w_br_m"] = W_br_m.astype(BF16)
    wp["w_out"] = W_out.astype(BF16)
    wp["ln1_g"] = ln1_g.reshape(1, D_MODEL)
    wp["ln1_b"] = ln1_b.reshape(1, D_MODEL)
    wp["w_gate"] = w_gate.astype(BF16)
    wp["w_up"] = w_up.astype(BF16)
    wp["w_down"] = w_down.astype(BF16)
    wp["ln2_g"] = ln2_g.reshape(1, D_MODEL)
    wp["ln2_b"] = ln2_b.reshape(1, D_MODEL)
    return wp


def kernel(x_prompt, x_sample, cache_mla_latent, cache_mla_krope, cache_fox_k, cache_fox_v, cache_fox_logf, cache_mem_k, cache_mem_v, page_table, mem_prompt, W_in, b_forget, b_gate, q_norm_g, W_uq, kv_norm_g, W_uk, W_uv, W_mem_k, W_mem_v, W_br_a, W_br_f, W_br_m, W_out, ln1_g, ln1_b, w_gate, w_up, w_down, ln2_g, ln2_b):
    wp = _prepare_weights(W_in, b_forget, b_gate, q_norm_g, W_uq, kv_norm_g, W_uk, W_uv,
                          W_br_a, W_br_f, W_br_m, W_out, ln1_g, ln1_b, w_gate, w_up, w_down, ln2_g, ln2_b)
    bp, sp, _ = x_prompt.shape
    bs, ss, _ = x_sample.shape
    n_pool, page, _ = cache_mla_latent.shape
    n_mem = mem_prompt.shape[1]
    past_len = page_table.shape[1] * page

    xp = x_prompt.reshape(bp * sp, D_MODEL)
    cos_p, sin_p = _rope_tables(jnp.arange(sp, dtype=jnp.int32))
    (lat_p, krope_p, fk_p, fv_p, logf_p, qcat_p, kcat_p, fq_p, fk16_p, fv16_p, mq_p) = _front(xp, cos_p, sin_p, wp)
    w_mem = jnp.concatenate([W_mem_k.reshape(D_MODEL, H_M * HD_M),
                             W_mem_v.reshape(D_MODEL, H_M * HD_M)], axis=1).astype(BF16)
    mem_kv = _matmul(mem_prompt.reshape(bp * n_mem, D_MODEL), w_mem)
    mem_k_p = mem_kv[:, :H_M * HD_M].reshape(bp, n_mem, H_M * HD_M)
    mem_v_p = mem_kv[:, H_M * HD_M:].reshape(bp, n_mem, H_M * HD_M)

    logf_t = jnp.transpose(logf_p.reshape(bp, sp, H_F), (0, 2, 1))
    c = _cumsum_lanes(logf_t.reshape(bp * H_F, sp))
    c_row = c.reshape(bp, H_F // 2, 2, sp)
    c_col = jnp.broadcast_to(c.reshape(bp, H_F, sp, 1), (bp, H_F, sp, LANES))

    o_lat_p = _mla_flash(qcat_p.reshape(bp, sp, H_A * QCAT), kcat_p.reshape(bp, sp, QCAT))
    o_f_p = _fox_flash(fq_p.reshape(bp, sp, -1), fk16_p.reshape(bp, sp, -1), fv16_p.reshape(bp, sp, -1),
                       c_col, c_row)
    o_m_p = _mem_attn(mq_p.reshape(bp, sp, -1), mem_k_p, mem_v_p)
    h_p = _merge(xp, o_lat_p.reshape(bp * sp, -1), o_f_p.reshape(bp * sp, -1), o_m_p.reshape(bp * sp, -1), wp)

    xs = x_sample.reshape(bs * ss, D_MODEL)
    tm_s = min(ROW_TILE, bs * ss)
    pos_s = past_len + (jnp.arange(tm_s, dtype=jnp.int32) % ss)
    cos_s, sin_s = _rope_tables(pos_s)
    (lat_s, krope_s, fk_s, fv_s, logf_s, qcat_s, _, fq_s, _, _, mq_s) = _front(xs, cos_s, sin_s, wp)
    pool_lf_t = _pool_suffix(jnp.transpose(cache_fox_logf, (0, 2, 1)))
    pool_kr_t = jnp.transpose(cache_mla_krope, (0, 2, 1))
    pool_fk_t = jnp.transpose(cache_fox_k, (0, 2, 3, 1)).reshape(n_pool, H_F * HD_F, page)
    pool_fv_t = jnp.transpose(cache_fox_v, (0, 2, 3, 1)).reshape(n_pool, H_F * HD_F, page)
    o_lat_s, o_f_s, y_p = _decode_attn(
        page_table, qcat_s.reshape(bs, ss * H_A, QCAT), fq_s.reshape(bs, ss, -1),
        lat_s.reshape(bs, ss, -1), krope_s.reshape(bs, ss, -1), fk_s.reshape(bs, ss, -1),
        fv_s.reshape(bs, ss, -1), logf_s.reshape(bs, ss, -1),
        cache_mla_latent, pool_kr_t, pool_fk_t, pool_fv_t, pool_lf_t, h_p, wp)
    o_m_s = _mem_decode(mq_s.reshape(bs, ss * H_M, HD_M), cache_mem_k.reshape(bs, n_mem * H_M, HD_M),
                        cache_mem_v.reshape(bs, n_mem * H_M, HD_M))
    h_s = _merge(xs, o_lat_s.reshape(bs * ss, -1), o_f_s.reshape(bs * ss, -1), o_m_s.reshape(bs * ss, -1), wp)
    y_s = _mlp(h_s, wp)

    return (y_p.reshape(bp, sp, D_MODEL), y_s.reshape(bs, ss, D_MODEL),
            lat_p.reshape(bp, sp, KV_LORA), krope_p.reshape(bp, sp, ROPE),
            fk_p.reshape(bp, sp, H_F, HD_F), fv_p.reshape(bp, sp, H_F, HD_F), logf_p.reshape(bp, sp, H_F),
            mem_k_p.reshape(bp, n_mem, H_M, HD_M), mem_v_p.reshape(bp, n_mem, H_M, HD_M),
            lat_s.reshape(bs, ss, KV_LORA), krope_s.reshape(bs, ss, ROPE),
            fk_s.reshape(bs, ss, H_F, HD_F), fv_s.reshape(bs, ss, H_F, HD_F), logf_s.reshape(bs, ss, H_F))
```

```python
import functools
import math

import jax
import jax.numpy as jnp
from jax import lax
from jax.experimental import pallas as pl
from jax.experimental.pallas import tpu as pltpu

F32 = jnp.float32
BF16 = jnp.bfloat16

H_A, NOPE, ROPE, V_A = 8, 128, 64, 128
Q_LORA, KV_LORA = 512, 256
H_F, HD_F = 8, 64
H_M, HD_M = 4, 128
D_MODEL = 1024
ROPE_THETA = 10000.0
LN_EPS = 1e-5
RMS_EPS = 1e-6
DEPTH = 1
ALPHA = (2 * DEPTH) ** 0.25
MLA_SCALE = (NOPE + ROPE) ** -0.5
FOX_SCALE = HD_F ** -0.5
MEM_SCALE = HD_M ** -0.5

LANES = 128
QCAT = KV_LORA + LANES
NEG_INF = float("-inf")

ROW_TILE = 256
ATT_TILE = 512
SOFTMAX_ROWS = 32
MLA_HEADS_PER_STEP = 8
FOX_PAIRS_PER_STEP = 4
MEM_DECODE_SEQS_PER_STEP = 4
CUMSUM_CHUNK = 256
DECODE_PAGES_PER_GROUP = 8
POOL_PAGES_PER_STEP = 256
MLP_UP_PIECES = 5
MLP_DOWN_PIECES = 3


def _dot(a, b):
    return jnp.dot(a, b, preferred_element_type=F32)


def _dot_nt(a, b):
    return lax.dot_general(a, b, (((1,), (1,)), ((), ())), preferred_element_type=F32)


def _const_spec(shape):
    zeros = (0,) * len(shape)
    return pl.BlockSpec(shape, lambda *_: zeros, pipeline_mode=pl.Buffered(1))


def _split3(x):
    hi = x.astype(BF16).astype(F32)
    r = x - hi
    mid = r.astype(BF16).astype(F32)
    lo = (r - mid).astype(BF16).astype(F32)
    return hi, mid, lo


def _exact_dot_ones(x, tri):
    hi, mid, lo = _split3(x)
    return _dot(hi, tri) + _dot(mid, tri) + _dot(lo, tri)


def _layernorm(v, g, b):
    mu = jnp.mean(v, axis=-1, keepdims=True)
    d = v - mu
    var = jnp.mean(d * d, axis=-1, keepdims=True)
    return d * lax.rsqrt(var + LN_EPS) * g + b


def _rmsnorm(v, g):
    return v * lax.rsqrt(jnp.mean(v * v, axis=-1, keepdims=True) + RMS_EPS) * g


def _sigmoid(v):
    return 1.0 / (1.0 + jnp.exp(-v))


def _rope_tile(x, cos, sin_signed):
    lane = lax.broadcasted_iota(jnp.int32, x.shape, 1)
    rot = jnp.where(lane < ROPE // 2, pltpu.roll(x, LANES - ROPE // 2, 1), pltpu.roll(x, ROPE // 2, 1))
    return x * cos + rot * sin_signed


def _front_kernel(x_ref, cos_ref, sin_ref, w_ref, wuq_ref, wuk_ref, qg_ref, kvg_ref, bf_ref,
                  lat_ref, krope_ref, fk_ref, fv_ref, logf_ref,
                  qcat_ref, kcat_ref, fq_ref, fk16_ref, fv16_ref, mq_ref):
    xb = x_ref[...].astype(BF16)
    z = _dot(xb, w_ref[...])
    cos = cos_ref[...]
    sin = sin_ref[...]
    qn = _rmsnorm(z[:, 0:512], qg_ref[...]).astype(BF16)
    q = _dot(qn, wuq_ref[...])
    for h in range(H_A):
        q_nope = q[:, h * NOPE:(h + 1) * NOPE].astype(BF16)
        qcat_ref[:, h * QCAT:h * QCAT + KV_LORA] = _dot(q_nope, wuk_ref[h]).astype(BF16)
        xr = q[:, H_A * NOPE + h * LANES:H_A * NOPE + (h + 1) * LANES]
        qcat_ref[:, h * QCAT + KV_LORA:(h + 1) * QCAT] = _rope_tile(xr, cos, sin).astype(BF16)
    lat = _rmsnorm(z[:, 512:768], kvg_ref[...])
    lat_ref[...] = lat
    kcat_ref[:, 0:KV_LORA] = lat.astype(BF16)
    kr = _rope_tile(z[:, 2816:2944], cos, sin)
    krope_ref[...] = kr[:, 0:ROPE]
    kcat_ref[:, KV_LORA:QCAT] = kr.astype(BF16)
    fq_ref[...] = (z[:, 768:1280] * FOX_SCALE).astype(BF16)
    zfk = z[:, 1280:1792]
    zfv = z[:, 1792:2304]
    fk_ref[...] = zfk
    fv_ref[...] = zfv
    fk16_ref[...] = zfk.astype(BF16)
    fv16_ref[...] = zfv.astype(BF16)
    mq_ref[...] = z[:, 2304:2816].astype(BF16)
    zf = z[:, 2944:3072] + bf_ref[...]
    logf = jnp.minimum(zf, 0.0) - jnp.log1p(jnp.exp(-jnp.abs(zf)))
    logf_ref[...] = logf[:, 0:H_F]


def _front(x2d, cos_t, sin_t, wp):
    t = x2d.shape[0]
    tm = min(ROW_TILE, t)
    nblk = t // tm
    ntab = cos_t.shape[0] // tm
    row = lambda n: pl.BlockSpec((tm, n), lambda i: (i, 0))
    tab = pl.BlockSpec((tm, LANES), lambda i: (i % ntab, 0))
    out_shape = (
        jax.ShapeDtypeStruct((t, KV_LORA), F32), jax.ShapeDtypeStruct((t, ROPE), F32),
        jax.ShapeDtypeStruct((t, H_F * HD_F), F32), jax.ShapeDtypeStruct((t, H_F * HD_F), F32),
        jax.ShapeDtypeStruct((t, H_F), F32),
        jax.ShapeDtypeStruct((t, H_A * QCAT), BF16), jax.ShapeDtypeStruct((t, QCAT), BF16),
        jax.ShapeDtypeStruct((t, H_F * HD_F), BF16), jax.ShapeDtypeStruct((t, H_F * HD_F), BF16),
        jax.ShapeDtypeStruct((t, H_F * HD_F), BF16), jax.ShapeDtypeStruct((t, H_M * HD_M), BF16),
    )
    out_specs = (row(KV_LORA), row(ROPE), row(512), row(512), row(H_F),
                 row(H_A * QCAT), row(QCAT), row(512), row(512), row(512), row(512))
    return pl.pallas_call(
        _front_kernel,
        grid=(nblk,),
        in_specs=[row(D_MODEL), tab, tab,
                  _const_spec(wp["w_front"].shape), _const_spec(wp["w_uq"].shape),
                  _const_spec(wp["w_uk"].shape), _const_spec((1, Q_LORA)),
                  _const_spec((1, KV_LORA)), _const_spec((1, LANES))],
        out_specs=out_specs,
        out_shape=out_shape,
        compiler_params=pltpu.CompilerParams(dimension_semantics=("parallel",)),
        name="front",
    )(x2d, cos_t, sin_t, wp["w_front"], wp["w_uq"], wp["w_uk"], wp["q_g"], wp["kv_g"], wp["b_f"])


def _mm_kernel(x_ref, w_ref, o_ref):
    o_ref[...] = _dot(x_ref[...].astype(BF16), w_ref[...])


def _matmul(x2d, w):
    t, k = x2d.shape
    n = w.shape[1]
    tm = min(ROW_TILE, t)
    return pl.pallas_call(
        _mm_kernel,
        grid=(t // tm,),
        in_specs=[pl.BlockSpec((tm, k), lambda i: (i, 0)), _const_spec((k, n))],
        out_specs=pl.BlockSpec((tm, n), lambda i: (i, 0)),
        out_shape=jax.ShapeDtypeStruct((t, n), F32),
        compiler_params=pltpu.CompilerParams(dimension_semantics=("parallel",)),
        name="mem_kv_proj",
    )(x2d, w)


def _cumsum_kernel(x_ref, o_ref):
    n = x_ref.shape[1]
    ck = min(CUMSUM_CHUNK, n)
    r = lax.broadcasted_iota(jnp.int32, (ck, ck), 0)
    c = lax.broadcasted_iota(jnp.int32, (ck, ck), 1)
    tri = (r <= c).astype(F32)
    carry = jnp.zeros((x_ref.shape[0], 1), F32)
    for j in range(n // ck):
        cs = _exact_dot_ones(x_ref[:, j * ck:(j + 1) * ck], tri) + carry
        o_ref[:, j * ck:(j + 1) * ck] = cs
        carry = cs[:, ck - 1:ck]


def _cumsum_lanes(x2d):
    return pl.pallas_call(
        _cumsum_kernel,
        out_shape=jax.ShapeDtypeStruct(x2d.shape, F32),
        name="logf_cumsum",
    )(x2d)


def _lane_tile(x, width):
    n = width // LANES
    return x if n == 1 else jnp.concatenate([x] * n, axis=1)


def _softmax_rows(s_ref, p_ref, m_ref, l_ref, a_ref, idx, logit_fn, diagonal):
    tq, tk = s_ref.shape[1:]
    assert not diagonal or tq == tk
    rc = min(SOFTMAX_ROWS, tq)
    for c in range(tq // rc):
        rows = slice(c * rc, (c + 1) * rc)
        w = min(tk, pl.cdiv((c + 1) * rc, LANES) * LANES) if diagonal else tk
        s = logit_fn(s_ref[idx, rows, 0:w], rows, w)
        if diagonal:
            qpos = c * rc + lax.broadcasted_iota(jnp.int32, (rc, w), 0)
            kpos = lax.broadcasted_iota(jnp.int32, (rc, w), 1)
            s = jnp.where(kpos <= qpos, s, NEG_INF)
        m_prev = m_ref[idx, rows, :]
        m_new = jnp.maximum(m_prev, jnp.max(s, axis=-1, keepdims=True))
        alpha = jnp.exp(m_prev - m_new)
        p = jnp.exp(s - _lane_tile(m_new, w))
        l_ref[idx, rows, :] = alpha * l_ref[idx, rows, :] + jnp.sum(p, axis=-1, keepdims=True)
        m_ref[idx, rows, :] = m_new
        a_ref[idx, rows, :] = alpha
        p_ref[idx, rows, 0:w] = p.astype(BF16)
        if w < tk:
            p_ref[idx, rows, w:tk] = jnp.zeros((rc, tk - w), BF16)


def _flash_init(ki, m_ref, l_ref, acc_ref):
    @pl.when(ki == 0)
    def _():
        m_ref[...] = jnp.full(m_ref.shape, NEG_INF, F32)
        l_ref[...] = jnp.zeros(l_ref.shape, F32)
        acc_ref[...] = jnp.zeros(acc_ref.shape, F32)


def _causal_steps(n):
    pairs = [(qi, ki) for qi in range(n) for ki in range(qi + 1)]
    return (jnp.array([p[0] for p in pairs], jnp.int32), jnp.array([p[1] for p in pairs], jnp.int32))


def _flash_scratch(nh, t, width):
    return [pltpu.VMEM((nh, t, t), F32), pltpu.VMEM((nh, t, t), BF16),
            pltpu.VMEM((nh, t, LANES), F32), pltpu.VMEM((nh, t, LANES), F32),
            pltpu.VMEM((nh, t, LANES), F32), pltpu.VMEM((nh, t, width), F32)]


def _mla_flash_kernel(qt_ref, kt_ref, q_ref, k_ref, o_ref, s_ref, p_ref, m_ref, l_ref, a_ref, acc_ref):
    step_id = pl.program_id(2)
    qi = qt_ref[step_id]
    ki = kt_ref[step_id]
    tq = q_ref.shape[1]
    tk = k_ref.shape[1]
    nh = s_ref.shape[0]
    _flash_init(ki, m_ref, l_ref, acc_ref)

    def step(masked):
        k = k_ref[0]
        for g in range(nh):
            s_ref[g] = _dot_nt(q_ref[0, :, g * QCAT:(g + 1) * QCAT], k)
        for g in range(nh):
            _softmax_rows(s_ref, p_ref, m_ref, l_ref, a_ref, g, lambda s, rows, w: s * MLA_SCALE, masked)
            acc_ref[g] = _lane_tile(a_ref[g], KV_LORA) * acc_ref[g] + _dot(p_ref[g], k[:, 0:KV_LORA])

    @pl.when(ki < qi)
    def _():
        step(False)

    @pl.when(ki == qi)
    def _():
        step(True)
        for g in range(nh):
            o_ref[0, :, g * KV_LORA:(g + 1) * KV_LORA] = (
                acc_ref[g] / _lane_tile(l_ref[g], KV_LORA)).astype(o_ref.dtype)


def _mla_flash(qcat, kcat):
    b, s, _ = qcat.shape
    t = min(ATT_TILE, s)
    qt, kt = _causal_steps(s // t)
    hg = MLA_HEADS_PER_STEP
    grid_spec = pltpu.PrefetchScalarGridSpec(
        num_scalar_prefetch=2,
        grid=(b, H_A // hg, qt.shape[0]),
        in_specs=[pl.BlockSpec((1, t, hg * QCAT), lambda b, h, i, qt, kt: (b, qt[i], h)),
                  pl.BlockSpec((1, t, QCAT), lambda b, h, i, qt, kt: (b, kt[i], 0))],
        out_specs=pl.BlockSpec((1, t, hg * KV_LORA), lambda b, h, i, qt, kt: (b, qt[i], h)),
        scratch_shapes=_flash_scratch(hg, t, KV_LORA),
    )
    return pl.pallas_call(
        _mla_flash_kernel,
        grid_spec=grid_spec,
        out_shape=jax.ShapeDtypeStruct((b, s, H_A * KV_LORA), BF16),
        compiler_params=pltpu.CompilerParams(dimension_semantics=("parallel", "parallel", "arbitrary")),
        name="mla_prompt_attn",
    )(qt, kt, qcat, kcat)


def _fox_flash_kernel(qt_ref, kt_ref, q_ref, k_ref, v_ref, cq_ref, ck_ref, o_ref,
                      s_ref, p_ref, m_ref, l_ref, a_ref, acc_ref):
    step_id = pl.program_id(2)
    qi = qt_ref[step_id]
    ki = kt_ref[step_id]
    tq = q_ref.shape[1]
    tk = k_ref.shape[1]
    npair = s_ref.shape[0] // 2
    _flash_init(ki, m_ref, l_ref, acc_ref)

    def step(masked):
        lane = lax.broadcasted_iota(jnp.int32, (tq, LANES), 1)
        for pr in range(npair):
            q2 = q_ref[0, :, pr * LANES:(pr + 1) * LANES]
            k2 = k_ref[0, :, pr * LANES:(pr + 1) * LANES]
            for hh in range(2):
                keep = (lane >= HD_F) if hh else (lane < HD_F)
                s_ref[2 * pr + hh] = _dot_nt(jnp.where(keep, q2, jnp.zeros_like(q2)), k2)
        for pr in range(npair):
            v2 = v_ref[0, :, pr * LANES:(pr + 1) * LANES]
            for hh in range(2):
                h = 2 * pr + hh

                def logits(s, rows, w, h=h, pr=pr, hh=hh):
                    return s + _lane_tile(cq_ref[0, h, rows, :], w) - ck_ref[0, pr, hh:hh + 1, 0:w]
                _softmax_rows(s_ref, p_ref, m_ref, l_ref, a_ref, h, logits, masked)
                acc_ref[h] = a_ref[h] * acc_ref[h] + _dot(p_ref[h], v2)

    @pl.when(ki < qi)
    def _():
        step(False)

    @pl.when(ki == qi)
    def _():
        step(True)
        lane = lax.broadcasted_iota(jnp.int32, (tq, LANES), 1)
        for pr in range(npair):
            o0 = acc_ref[2 * pr] / l_ref[2 * pr]
            o1 = acc_ref[2 * pr + 1] / l_ref[2 * pr + 1]
            o_ref[0, :, pr * LANES:(pr + 1) * LANES] = jnp.where(lane < HD_F, o0, o1).astype(o_ref.dtype)


def _fox_flash(fq, fk16, fv16, c_col, c_row):
    b, s, _ = fq.shape
    t = min(ATT_TILE, s)
    qt, kt = _causal_steps(s // t)
    pp = FOX_PAIRS_PER_STEP
    qspec = pl.BlockSpec((1, t, pp * LANES), lambda b, h, i, qt, kt: (b, qt[i], h))
    kspec = pl.BlockSpec((1, t, pp * LANES), lambda b, h, i, qt, kt: (b, kt[i], h))
    grid_spec = pltpu.PrefetchScalarGridSpec(
        num_scalar_prefetch=2,
        grid=(b, H_F // (2 * pp), qt.shape[0]),
        in_specs=[qspec, kspec, kspec,
                  pl.BlockSpec((1, 2 * pp, t, LANES), lambda b, h, i, qt, kt: (b, h, qt[i], 0)),
                  pl.BlockSpec((1, pp, 2, t), lambda b, h, i, qt, kt: (b, h, 0, kt[i]))],
        out_specs=qspec,
        scratch_shapes=_flash_scratch(2 * pp, t, LANES),
    )
    return pl.pallas_call(
        _fox_flash_kernel,
        grid_spec=grid_spec,
        out_shape=jax.ShapeDtypeStruct((b, s, H_F * HD_F), BF16),
        compiler_params=pltpu.CompilerParams(dimension_semantics=("parallel", "parallel", "arbitrary")),
        name="fox_prompt_attn",
    )(qt, kt, fq, fk16, fv16, c_col, c_row)


def _mem_attn_kernel(q_ref, k_ref, v_ref, o_ref, s_ref, p_ref, l_ref):
    tq = q_ref.shape[1]
    rc = min(SOFTMAX_ROWS, tq)
    heads = [slice(h * HD_M, (h + 1) * HD_M) for h in range(H_M)]
    for h, hs in enumerate(heads):
        s_ref[h] = _dot_nt(q_ref[0, :, hs], k_ref[0, :, hs].astype(BF16))
    for h, hs in enumerate(heads):
        for c in range(tq // rc):
            rows = slice(c * rc, (c + 1) * rc)
            s = s_ref[h, rows, :] * MEM_SCALE
            p = jnp.exp(s - jnp.max(s, axis=-1, keepdims=True))
            l_ref[h, rows, :] = jnp.broadcast_to(jnp.sum(p, axis=-1, keepdims=True), (rc, HD_M))
            p_ref[h, rows, :] = p.astype(BF16)
        o_ref[0, :, hs] = (_dot(p_ref[h], v_ref[0, :, hs].astype(BF16)) / l_ref[h]).astype(o_ref.dtype)


def _mem_attn(mq, mem_k, mem_v):
    b, s, d = mq.shape
    nm = mem_k.shape[1]
    t = min(ATT_TILE, s)
    return pl.pallas_call(
        _mem_attn_kernel,
        grid=(b, s // t),
        in_specs=[pl.BlockSpec((1, t, d), lambda b, qi: (b, qi, 0)),
                  pl.BlockSpec((1, nm, d), lambda b, qi: (b, 0, 0)),
                  pl.BlockSpec((1, nm, d), lambda b, qi: (b, 0, 0))],
        out_specs=pl.BlockSpec((1, t, d), lambda b, qi: (b, qi, 0)),
        out_shape=jax.ShapeDtypeStruct((b, s, d), BF16),
        scratch_shapes=[pltpu.VMEM((H_M, t, nm), F32), pltpu.VMEM((H_M, t, nm), BF16),
                        pltpu.VMEM((H_M, t, HD_M), F32)],
        compiler_params=pltpu.CompilerParams(dimension_semantics=("parallel", "parallel")),
        name="mem_prompt_attn",
    )(mq, mem_k, mem_v)


def _merge_kernel(x_ref, olat_ref, of_ref, om_ref, wg_ref, bg_ref, wuv_ref, wa_ref, wf_ref, wm_ref,
                  wo_ref, g_ref, b_ref, h_ref, oa_ref):
    x = x_ref[...]
    xb = x.astype(BF16)
    for h in range(H_A):
        oa_ref[:, h * V_A:(h + 1) * V_A] = _dot(
            olat_ref[:, h * KV_LORA:(h + 1) * KV_LORA], wuv_ref[h]).astype(BF16)
    branches = (_dot(oa_ref[...], wa_ref[...]), _dot(of_ref[...], wf_ref[...]),
                _dot(om_ref[...], wm_ref[...]))
    merged = None
    for i, br in enumerate(branches):
        gate = _sigmoid(_dot(xb, wg_ref[:, i * D_MODEL:(i + 1) * D_MODEL])
                        + bg_ref[:, i * D_MODEL:(i + 1) * D_MODEL])
        merged = gate * br if merged is None else merged + gate * br
    pre = ALPHA * x + _dot(merged.astype(BF16), wo_ref[...])
    h_ref[...] = _layernorm(pre, g_ref[...], b_ref[...])


def _merge(x2d, o_lat, o_f, o_m, wp):
    t = x2d.shape[0]
    tm = min(ROW_TILE, t)
    row = lambda n: pl.BlockSpec((tm, n), lambda i: (i, 0))
    names = ("w_g", "b_g", "w_uv", "w_br_a", "w_br_f", "w_br_m", "w_out", "ln1_g", "ln1_b")
    return pl.pallas_call(
        _merge_kernel,
        grid=(t // tm,),
        in_specs=[row(D_MODEL), row(H_A * KV_LORA), row(H_F * HD_F), row(H_M * HD_M)]
        + [_const_spec(wp[n].shape) for n in names],
        out_specs=row(D_MODEL),
        out_shape=jax.ShapeDtypeStruct((t, D_MODEL), F32),
        scratch_shapes=[pltpu.VMEM((tm, H_A * V_A), BF16)],
        compiler_params=pltpu.CompilerParams(dimension_semantics=("parallel",)),
        name="merge_ln1",
    )(x2d, o_lat, o_f, o_m, *[wp[n] for n in names])


def _merge_work_items(x_ref, olat_ref, of_ref, om_ref, wg_ref, bg_ref, wuv_ref, wa_ref, wf_ref, wm_ref,
                      wo_ref, g_ref, b_ref, h_ref, oa_ref, gate_ref, mg_ref):
    def heads():
        for h in range(H_A):
            oa_ref[:, h * V_A:(h + 1) * V_A] = _dot(
                olat_ref[:, h * KV_LORA:(h + 1) * KV_LORA], wuv_ref[h]).astype(BF16)

    def gate(i):
        cols = slice(i * D_MODEL, (i + 1) * D_MODEL)
        gate_ref[...] = _sigmoid(_dot(x_ref[...].astype(BF16), wg_ref[:, cols]) + bg_ref[:, cols])

    def branch(i, o_ref, w_ref):
        br = gate_ref[...] * _dot(o_ref[...], w_ref[...])
        mg_ref[...] = br if i == 0 else mg_ref[...] + br

    def out():
        pre = ALPHA * x_ref[...] + _dot(mg_ref[...].astype(BF16), wo_ref[...])
        h_ref[...] = _layernorm(pre, g_ref[...], b_ref[...])

    items = [(H_A * KV_LORA * V_A, heads)]
    for i, (o_ref, w_ref) in enumerate(((oa_ref, wa_ref), (of_ref, wf_ref), (om_ref, wm_ref))):
        items.append((D_MODEL * D_MODEL, functools.partial(gate, i)))
        items.append((w_ref.shape[0] * w_ref.shape[1], functools.partial(branch, i, o_ref, w_ref)))
    items.append((D_MODEL * D_MODEL, out))
    return items


def _mlp_kernel(h_ref, wg_ref, wu_ref, wd_ref, g_ref, b_ref, y_ref, a_ref, *, n_chunks):
    h = h_ref[...]
    hb = h.astype(BF16)
    fc = wg_ref.shape[1] // n_chunks
    for c in range(n_chunks):
        gt = _dot(hb, wg_ref[:, c * fc:(c + 1) * fc])
        up = _dot(hb, wu_ref[:, c * fc:(c + 1) * fc])
        a_ref[:, c * fc:(c + 1) * fc] = (gt * _sigmoid(gt) * up).astype(BF16)
    y_ref[...] = _layernorm(ALPHA * h + _dot(a_ref[...], wd_ref[...]), g_ref[...], b_ref[...])


def _lane_tile_chunks(width, parts):
    tiles = width // LANES
    bounds = [LANES * ((i * tiles) // parts) for i in range(parts + 1)]
    return [(bounds[i], bounds[i + 1]) for i in range(parts) if bounds[i + 1] > bounds[i]]


def _mlp_work_items(h_ref, wg_ref, wu_ref, wd_ref, g_ref, b_ref, y_ref, a_ref, pre_ref):
    def up_piece(c0, c1):
        hb = h_ref[...].astype(BF16)
        gt = _dot(hb, wg_ref[:, c0:c1])
        up = _dot(hb, wu_ref[:, c0:c1])
        a_ref[:, c0:c1] = (gt * _sigmoid(gt) * up).astype(BF16)

    def down_piece(c0, c1, last):
        pre_ref[:, c0:c1] = _dot(a_ref[...], wd_ref[:, c0:c1])
        if last:
            y_ref[...] = _layernorm(ALPHA * h_ref[...] + pre_ref[...], g_ref[...], b_ref[...])

    d_model, d_ff = wg_ref.shape
    items = [(2 * d_model * (c1 - c0), functools.partial(up_piece, c0, c1))
             for c0, c1 in _lane_tile_chunks(d_ff, MLP_UP_PIECES)]
    down = _lane_tile_chunks(wd_ref.shape[1], MLP_DOWN_PIECES)
    items += [(d_ff * (c1 - c0), functools.partial(down_piece, c0, c1, i == len(down) - 1))
              for i, (c0, c1) in enumerate(down)]
    return items


def _spread_work(items, n):
    total = sum(c for c, _ in items)
    slots = [[] for _ in range(n)]
    done = 0
    for cost, work in items:
        slots[min(n - 1, int((done + cost / 2) * n / total))].append(work)
        done += cost
    return slots


def _mlp(h2d, wp):
    t = h2d.shape[0]
    tm = min(ROW_TILE, t)
    d_ff = wp["w_gate"].shape[1]
    n_chunks = 2
    row = lambda n: pl.BlockSpec((tm, n), lambda i: (i, 0))
    names = ("w_gate", "w_up", "w_down", "ln2_g", "ln2_b")
    return pl.pallas_call(
        functools.partial(_mlp_kernel, n_chunks=n_chunks),
        grid=(t // tm,),
        in_specs=[row(D_MODEL)] + [_const_spec(wp[n].shape) for n in names],
        out_specs=row(D_MODEL),
        out_shape=jax.ShapeDtypeStruct((t, D_MODEL), F32),
        scratch_shapes=[pltpu.VMEM((tm, d_ff), BF16)],
        compiler_params=pltpu.CompilerParams(dimension_semantics=("parallel",)),
        name="mlp_ln2",
    )(h2d, *[wp[n] for n in names])


def _rows_per_query(x, nq):
    return jnp.concatenate([jnp.broadcast_to(x[q:q + 1], (8, x.shape[1])) for q in range(nq)], axis=0)


def _page_copies(pt_ref, pools, bufs, sem, b, group, slot, pg, npages):
    copies = []
    for i in range(pg):
        pid = pt_ref[b * npages + group * pg + i]
        for a, (pool, buf) in enumerate(zip(pools, bufs)):
            copy = pltpu.make_async_copy(pool.at[pid], buf.at[slot, i], sem.at[slot, a])
            copies.append((copy, i % 2))
    return copies


def _pool_suffix_kernel(x_ref, o_ref):
    pb = x_ref.shape[0]
    l = x_ref[...].reshape(pb * H_F, LANES)
    r = lax.broadcasted_iota(jnp.int32, (LANES, LANES), 0)
    c = lax.broadcasted_iota(jnp.int32, (LANES, LANES), 1)
    excl = _exact_dot_ones(l, (r > c).astype(F32))
    tot = jnp.broadcast_to(excl[:, 0:1] + l[:, 0:1], excl.shape)
    o_ref[:, 0:H_F, :] = excl.reshape(pb, H_F, LANES)
    o_ref[:, H_F:2 * H_F, :] = tot.reshape(pb, H_F, LANES)


def _pool_suffix(logf_t):
    n_pool = logf_t.shape[0]
    pb = min(POOL_PAGES_PER_STEP, n_pool)
    return pl.pallas_call(
        _pool_suffix_kernel,
        grid=(n_pool // pb,),
        in_specs=[pl.BlockSpec((pb, H_F, LANES), lambda i: (i, 0, 0))],
        out_specs=pl.BlockSpec((pb, 2 * H_F, LANES), lambda i: (i, 0, 0)),
        out_shape=jax.ShapeDtypeStruct((n_pool, 2 * H_F, LANES), F32),
        compiler_params=pltpu.CompilerParams(dimension_semantics=("parallel",)),
        name="pool_logf_suffix",
    )(logf_t)


def _decode_kernel(pt_ref, qcat_ref, fq_ref, latn_ref, krn_ref, fkn_ref, fvn_ref, lfn_ref,
                   lat_hbm, kr_hbm, fk_hbm, fv_hbm, lf_hbm,
                   x_ref, polat_ref, pof_ref, pom_ref, wgate_ref, bgate_ref, wuv_ref, wa_ref, wf_ref, wm_ref,
                   wo_ref, g1_ref, b1_ref, wg_ref, wu_ref, wd_ref, g2_ref, b2_ref,
                   olat_ref, of_ref, y_ref,
                   lat_buf, kr_buf, fk_buf, fv_buf, lf_buf, sem,
                   pm_ref, pl_ref, pacca_ref, paccf_ref,
                   h_ref, oa_ref, gate_ref, mg_ref, a_ref, pre_ref, *, pg, ng, nq, npages):
    row_work = _spread_work(
        _merge_work_items(x_ref, polat_ref, pof_ref, pom_ref, wgate_ref, bgate_ref, wuv_ref, wa_ref,
                          wf_ref, wm_ref, wo_ref, g1_ref, b1_ref, h_ref, oa_ref, gate_ref, mg_ref)
        + _mlp_work_items(h_ref, wg_ref, wu_ref, wd_ref, g2_ref, b2_ref, y_ref, a_ref, pre_ref), ng)
    b = pl.program_id(0)
    nb = pl.num_programs(0)
    nrow = nq * 8
    page = lat_buf.shape[2]
    pools = (lat_hbm, kr_hbm, fk_hbm, fv_hbm, lf_hbm)
    bufs = (lat_buf, kr_buf, fk_buf, fv_buf, lf_buf)
    copies = functools.partial(_page_copies, pt_ref, pools, bufs, sem, pg=pg, npages=npages)

    @pl.when(b == 0)
    def _():
        for c, prio in copies(b, ng - 1, 0):
            c.start(priority=prio)

    row512 = lax.broadcasted_iota(jnp.int32, (nrow, H_F * HD_F), 0)
    lane512 = lax.broadcasted_iota(jnp.int32, (nrow, H_F * HD_F), 1)
    head_mask = (lane512 // HD_F) == (row512 % 8)

    def store_partial(idx, k, m, l, acc, acc_ref):
        pm_ref[idx, k] = jnp.broadcast_to(m, (nrow, LANES))
        pl_ref[idx, k] = jnp.broadcast_to(l, (nrow, LANES))
        acc_ref[k] = acc

    qrow = lax.broadcasted_iota(jnp.int32, (nrow, 1), 0) // 8
    qbd = jnp.where(head_mask, _rows_per_query(fq_ref[0].astype(F32), nq), 0.0)
    qbd16 = qbd.astype(BF16)
    lfn = lfn_ref[0]
    r8 = lax.broadcasted_iota(jnp.int32, (nrow, H_F), 0)
    l8 = lax.broadcasted_iota(jnp.int32, (nrow, H_F), 1)
    pick = l8 == (r8 % 8)
    cum = []
    for q in range(nq):
        cum.append(lfn[q:q + 1] if q == 0 else cum[-1] + lfn[q:q + 1])
    cq = jnp.sum(jnp.where(pick, jnp.concatenate(
        [jnp.broadcast_to(c, (8, H_F)) for c in cum], axis=0), 0.0), axis=1, keepdims=True)
    fkn = fkn_ref[0]
    fvn = fvn_ref[0]
    sf = []
    for s in range(nq):
        ck = jnp.sum(jnp.where(pick, jnp.broadcast_to(cum[s], (nrow, H_F)), 0.0),
                     axis=1, keepdims=True)
        v = jnp.sum(qbd * fkn[s:s + 1], axis=1, keepdims=True) + cq - ck
        sf.append(jnp.where(qrow >= s, v, NEG_INF))
    qc16 = qcat_ref[0]
    qc = qc16.astype(F32)
    latn = latn_ref[0]
    krn = krn_ref[0]
    sa = []
    for s in range(nq):
        v = (jnp.sum(qc[:, 0:KV_LORA] * latn[s:s + 1], axis=1, keepdims=True)
             + jnp.sum(qc[:, KV_LORA:KV_LORA + ROPE] * krn[s:s + 1], axis=1, keepdims=True))
        sa.append(jnp.where(qrow >= s, v * MLA_SCALE, NEG_INF))
    for idx, (sc, vals, acc_ref) in enumerate(((sa, latn, pacca_ref), (sf, fvn, paccf_ref))):
        m = sc[0]
        for s in range(1, nq):
            m = jnp.maximum(m, sc[s])
        l = jnp.zeros_like(m)
        acc = jnp.zeros(acc_ref.shape[1:], F32)
        for s in range(nq):
            p = jnp.exp(sc[s] - m)
            l = l + p
            acc = acc + p * vals[s:s + 1]
        store_partial(idx, ng, m, l, acc, acc_ref)

    carry = jnp.zeros((H_F, LANES), F32)
    for k in range(ng):
        slot = k % 2
        group = ng - 1 - k
        if k + 1 < ng:
            for c, prio in copies(b, group - 1, 1 - slot):
                c.start(priority=prio)
        else:
            @pl.when(b + 1 < nb)
            def _():
                for c, prio in copies(b + 1, ng - 1, 1 - slot):
                    c.start(priority=prio)
        for work in row_work[k]:
            work()
        for c, _ in copies(b, group, slot):
            c.wait()

        pieces = [None] * pg
        for i in reversed(range(pg)):
            sums = lf_buf[slot, i]
            pieces[i] = sums[0:H_F] + carry
            carry = carry + sums[H_F:2 * H_F]
        bias = jnp.concatenate(pieces, axis=1)
        bias = jnp.concatenate([bias] * nq, axis=0)

        lat16 = [lat_buf[slot, i].astype(BF16) for i in range(pg)]
        sf = jnp.concatenate([_dot(qbd16, fk_buf[slot, i].astype(BF16)) for i in range(pg)], axis=1)
        sa = jnp.concatenate(
            [_dot_nt(qc16[:, 0:KV_LORA], lat16[i])
             + _dot(qc16[:, KV_LORA:KV_LORA + ROPE], kr_buf[slot, i].astype(BF16)) for i in range(pg)], axis=1)

        sf = sf + bias + cq
        m = jnp.max(sf, axis=-1, keepdims=True)
        p = jnp.exp(sf - m)
        p16 = p.astype(BF16)
        acc = _dot_nt(p16[:, 0:page], fv_buf[slot, 0].astype(BF16))
        for i in range(1, pg):
            acc = acc + _dot_nt(p16[:, i * page:(i + 1) * page], fv_buf[slot, i].astype(BF16))
        store_partial(1, k, m, jnp.sum(p, axis=-1, keepdims=True), acc, paccf_ref)

        sa = sa * MLA_SCALE
        m = jnp.max(sa, axis=-1, keepdims=True)
        p = jnp.exp(sa - m)
        p16 = p.astype(BF16)
        acc = _dot(p16[:, 0:page], lat16[0])
        for i in range(1, pg):
            acc = acc + _dot(p16[:, i * page:(i + 1) * page], lat16[i])
        store_partial(0, k, m, jnp.sum(p, axis=-1, keepdims=True), acc, pacca_ref)

    outs = []
    for idx, acc_ref in enumerate((pacca_ref, paccf_ref)):
        width = acc_ref.shape[2]
        m = pm_ref[idx, 0]
        for k in range(1, ng + 1):
            m = jnp.maximum(m, pm_ref[idx, k])
        l = jnp.zeros((nrow, LANES), F32)
        acc = jnp.zeros((nrow, width), F32)
        for k in range(ng + 1):
            w = jnp.exp(pm_ref[idx, k] - m)
            l = l + w * pl_ref[idx, k]
            acc = acc + _lane_tile(w, width) * acc_ref[k]
        outs.append(acc / _lane_tile(l, width))
    olat_ref[0] = outs[0].astype(olat_ref.dtype)
    of = jnp.where(head_mask, outs[1], 0.0)
    of_ref[0] = jnp.sum(of.reshape(nq, 8, H_F * HD_F), axis=1).astype(of_ref.dtype)


def _decode_attn(page_table, qcat_s, fq_s, lat_s, krope_s, fk_s, fv_s, logf_s,
                 pool_lat, pool_kr, pool_fk, pool_fv, pool_lf, x_prompt, o_lat_p, o_f_p, o_m_p, wp):
    nb, npages = page_table.shape
    nq = fq_s.shape[1]
    page = pool_lat.shape[1]
    pg = min(DECODE_PAGES_PER_GROUP, npages)
    ng = npages // pg
    assert ng * pg == npages and ng % 2 == 0, "page groups must alternate between the two buffer slots"
    nrow = nq * 8
    d_f = H_F * HD_F
    d_ff = wp["w_gate"].shape[1]
    t_p = x_prompt.shape[0]
    rows = t_p // nb
    assert rows * nb == t_p and rows % 8 == 0
    pt_flat = page_table.reshape(-1)

    def per_b(shape):
        return pl.BlockSpec((1,) + shape, lambda b, pt: (b,) + (0,) * len(shape))

    def row_block(n):
        return pl.BlockSpec((rows, n), lambda b, pt: (b, 0))

    hbm = pl.BlockSpec(memory_space=pl.ANY)
    pools = (pool_lat, pool_kr, pool_fk, pool_fv, pool_lf)
    row_inputs = (x_prompt, o_lat_p, o_f_p, o_m_p)
    w_names = ("w_g", "b_g", "w_uv", "w_br_a", "w_br_f", "w_br_m", "w_out", "ln1_g", "ln1_b",
               "w_gate", "w_up", "w_down", "ln2_g", "ln2_b")
    grid_spec = pltpu.PrefetchScalarGridSpec(
        num_scalar_prefetch=1,
        grid=(nb,),
        in_specs=[per_b((nrow, QCAT)), per_b((nq, d_f)), per_b((nq, KV_LORA)), per_b((nq, ROPE)),
                  per_b((nq, d_f)), per_b((nq, d_f)), per_b((nq, H_F))] + [hbm] * len(pools)
        + [row_block(a.shape[1]) for a in row_inputs] + [_const_spec(wp[n].shape) for n in w_names],
        out_specs=(per_b((nrow, KV_LORA)), per_b((nq, d_f)), row_block(D_MODEL)),
        scratch_shapes=[pltpu.VMEM((2, pg) + arr.shape[1:], F32) for arr in pools] + [
            pltpu.SemaphoreType.DMA((2, len(pools))),
            pltpu.VMEM((2, ng + 1, nrow, LANES), F32), pltpu.VMEM((2, ng + 1, nrow, LANES), F32),
            pltpu.VMEM((ng + 1, nrow, KV_LORA), F32), pltpu.VMEM((ng + 1, nrow, d_f), F32),
            pltpu.VMEM((rows, D_MODEL), F32), pltpu.VMEM((rows, H_A * V_A), BF16),
            pltpu.VMEM((rows, D_MODEL), F32), pltpu.VMEM((rows, D_MODEL), F32),
            pltpu.VMEM((rows, d_ff), BF16), pltpu.VMEM((rows, D_MODEL), F32),
        ],
    )
    return pl.pallas_call(
        functools.partial(_decode_kernel, pg=pg, ng=ng, nq=nq, npages=npages),
        grid_spec=grid_spec,
        out_shape=(jax.ShapeDtypeStruct((nb, nrow, KV_LORA), BF16),
                   jax.ShapeDtypeStruct((nb, nq, d_f), BF16),
                   jax.ShapeDtypeStruct((t_p, D_MODEL), F32)),
        compiler_params=pltpu.CompilerParams(dimension_semantics=("arbitrary",)),
        name="decode_attn_prompt_mlp",
    )(pt_flat, qcat_s, fq_s, lat_s, krope_s, fk_s, fv_s, logf_s, *pools,
      *row_inputs, *[wp[n] for n in w_names])


def _mem_decode_kernel(q_ref, k_ref, v_ref, o_ref):
    nseq, nrow, _ = q_ref.shape
    nkey = k_ref.shape[1]
    row = lax.broadcasted_iota(jnp.int32, (nrow, nkey), 0)
    col = lax.broadcasted_iota(jnp.int32, (nrow, nkey), 1)
    same_head = (col % H_M) == (row % H_M)
    for i in range(nseq):
        s = _dot_nt(q_ref[i], k_ref[i].astype(BF16)) * MEM_SCALE
        s = jnp.where(same_head, s, NEG_INF)
        m = jnp.max(s, axis=-1, keepdims=True)
        p = jnp.exp(s - m)
        l = jnp.sum(p, axis=-1, keepdims=True)
        o_ref[i] = (_dot(p.astype(BF16), v_ref[i].astype(BF16)) / l).astype(o_ref.dtype)


def _mem_decode(mq_s, mem_k, mem_v):
    nb, nrow, d = mq_s.shape
    nkey = mem_k.shape[1]
    nseq = min(MEM_DECODE_SEQS_PER_STEP, nb)
    return pl.pallas_call(
        _mem_decode_kernel,
        grid=(nb // nseq,),
        in_specs=[pl.BlockSpec((nseq, nrow, d), lambda b: (b, 0, 0)),
                  pl.BlockSpec((nseq, nkey, d), lambda b: (b, 0, 0)),
                  pl.BlockSpec((nseq, nkey, d), lambda b: (b, 0, 0))],
        out_specs=pl.BlockSpec((nseq, nrow, d), lambda b: (b, 0, 0)),
        out_shape=jax.ShapeDtypeStruct((nb, nrow, d), BF16),
        compiler_params=pltpu.CompilerParams(dimension_semantics=("parallel",)),
        name="mem_decode_attn",
    )(mq_s, mem_k, mem_v)


def _rope_tables(pos):
    half = ROPE // 2
    inv = jnp.exp(-math.log(ROPE_THETA) * jnp.arange(half, dtype=F32) / half)
    ang = pos.astype(F32)[:, None] * inv
    cos, sin = jnp.cos(ang), jnp.sin(ang)
    pad = jnp.zeros((pos.shape[0], LANES - ROPE), F32)
    return (jnp.concatenate([cos, cos, pad], axis=1), jnp.concatenate([-sin, sin, pad], axis=1))


def _prepare_weights(W_in, b_forget, b_gate, q_norm_g, W_uq, kv_norm_g, W_uk, W_uv,
                     W_br_a, W_br_f, W_br_m, W_out, ln1_g, ln1_b, w_gate, w_up, w_down, ln2_g, ln2_b):
    splits = (Q_LORA, KV_LORA, ROPE, H_F * HD_F, H_F * HD_F, H_F * HD_F, H_F, H_M * HD_M, 3 * D_MODEL)
    offs = [0]
    for n in splits:
        offs.append(offs[-1] + n)
    wq, wkv, wkr, wfq, wfk, wfv, wf, wmq, wg = (W_in[:, offs[i]:offs[i + 1]] for i in range(9))
    pad_to = lambda w: jnp.pad(w, ((0, 0), (0, LANES - w.shape[1])))
    wp = {}
    wp["w_front"] = jnp.concatenate([wq, wkv, wfq, wfk, wfv, wmq, pad_to(wkr), pad_to(wf)], axis=1).astype(BF16)
    w_nope = W_uq[:, :, :NOPE].reshape(Q_LORA, H_A * NOPE)
    w_rope = jnp.pad(W_uq[:, :, NOPE:], ((0, 0), (0, 0), (0, LANES - ROPE))).reshape(Q_LORA, H_A * LANES)
    wp["w_uq"] = jnp.concatenate([w_nope, w_rope], axis=1).astype(BF16)
    wp["w_uk"] = jnp.transpose(W_uk, (1, 2, 0)).astype(BF16)
    wp["w_uv"] = jnp.transpose(W_uv, (1, 0, 2)).astype(BF16)
    wp["q_g"] = q_norm_g.reshape(1, Q_LORA)
    wp["kv_g"] = kv_norm_g.reshape(1, KV_LORA)
    wp["b_f"] = jnp.pad(b_forget, (0, LANES - H_F)).reshape(1, LANES)
    wp["w_g"] = wg.astype(BF16)
    wp["b_g"] = b_gate.reshape(1, 3 * D_MODEL)
    wp["w_br_a"] = W_br_a.astype(BF16)
    wp["w_br_f"] = W_br_f.astype(BF16)
    wp["w_br_m"] = W_br_m.astype(BF16)
    wp["w_out"] = W_out.astype(BF16)
    wp["ln1_g"] = ln1_g.reshape(1, D_MODEL)
    wp["ln1_b"] = ln1_b.reshape(1, D_MODEL)
    wp["w_gate"] = w_gate.astype(BF16)
    wp["w_up"] = w_up.astype(BF16)
    wp["w_down"] = w_down.astype(BF16)
    wp["ln2_g"] = ln2_g.reshape(1, D_MODEL)
    wp["ln2_b"] = ln2_b.reshape(1, D_MODEL)
    return wp


def kernel(x_prompt, x_sample, cache_mla_latent, cache_mla_krope, cache_fox_k, cache_fox_v, cache_fox_logf, cache_mem_k, cache_mem_v, page_table, mem_prompt, W_in, b_forget, b_gate, q_norm_g, W_uq, kv_norm_g, W_uk, W_uv, W_mem_k, W_mem_v, W_br_a, W_br_f, W_br_m, W_out, ln1_g, ln1_b, w_gate, w_up, w_down, ln2_g, ln2_b):
    wp = _prepare_weights(W_in, b_forget, b_gate, q_norm_g, W_uq, kv_norm_g, W_uk, W_uv,
                          W_br_a, W_br_f, W_br_m, W_out, ln1_g, ln1_b, w_gate, w_up, w_down, ln2_g, ln2_b)
    bp, sp, _ = x_prompt.shape
    bs, ss, _ = x_sample.shape
    n_pool, page, _ = cache_mla_latent.shape
    n_mem = mem_prompt.shape[1]
    past_len = page_table.shape[1] * page

    xp = x_prompt.reshape(bp * sp, D_MODEL)
    cos_p, sin_p = _rope_tables(jnp.arange(sp, dtype=jnp.int32))
    (lat_p, krope_p, fk_p, fv_p, logf_p, qcat_p, kcat_p, fq_p, fk16_p, fv16_p, mq_p) = _front(xp, cos_p, sin_p, wp)
    w_mem = jnp.concatenate([W_mem_k.reshape(D_MODEL, H_M * HD_M),
                             W_mem_v.reshape(D_MODEL, H_M * HD_M)], axis=1).astype(BF16)
    mem_kv = _matmul(mem_prompt.reshape(bp * n_mem, D_MODEL), w_mem)
    mem_k_p = mem_kv[:, :H_M * HD_M].reshape(bp, n_mem, H_M * HD_M)
    mem_v_p = mem_kv[:, H_M * HD_M:].reshape(bp, n_mem, H_M * HD_M)

    logf_t = jnp.transpose(logf_p.reshape(bp, sp, H_F), (0, 2, 1))
    c = _cumsum_lanes(logf_t.reshape(bp * H_F, sp))
    c_row = c.reshape(bp, H_F // 2, 2, sp)
    c_col = jnp.broadcast_to(c.reshape(bp, H_F, sp, 1), (bp, H_F, sp, LANES))

    o_lat_p = _mla_flash(qcat_p.reshape(bp, sp, H_A * QCAT), kcat_p.reshape(bp, sp, QCAT))
    o_f_p = _fox_flash(fq_p.reshape(bp, sp, -1), fk16_p.reshape(bp, sp, -1), fv16_p.reshape(bp, sp, -1),
                       c_col, c_row)
    o_m_p = _mem_attn(mq_p.reshape(bp, sp, -1), mem_k_p, mem_v_p)

    xs = x_sample.reshape(bs * ss, D_MODEL)
    tm_s = min(ROW_TILE, bs * ss)
    pos_s = past_len + (jnp.arange(tm_s, dtype=jnp.int32) % ss)
    cos_s, sin_s = _rope_tables(pos_s)
    (lat_s, krope_s, fk_s, fv_s, logf_s, qcat_s, _, fq_s, _, _, mq_s) = _front(xs, cos_s, sin_s, wp)
    pool_lf_t = _pool_suffix(jnp.transpose(cache_fox_logf, (0, 2, 1)))
    pool_kr_t = jnp.transpose(cache_mla_krope, (0, 2, 1))
    pool_fk_t = jnp.transpose(cache_fox_k, (0, 2, 3, 1)).reshape(n_pool, H_F * HD_F, page)
    pool_fv_t = jnp.transpose(cache_fox_v, (0, 2, 3, 1)).reshape(n_pool, H_F * HD_F, page)
    o_lat_s, o_f_s, y_p = _decode_attn(
        page_table, qcat_s.reshape(bs, ss * H_A, QCAT), fq_s.reshape(bs, ss, -1),
        lat_s.reshape(bs, ss, -1), krope_s.reshape(bs, ss, -1), fk_s.reshape(bs, ss, -1),
        fv_s.reshape(bs, ss, -1), logf_s.reshape(bs, ss, -1),
        cache_mla_latent, pool_kr_t, pool_fk_t, pool_fv_t, pool_lf_t,
        xp, o_lat_p.reshape(bp * sp, -1), o_f_p.reshape(bp * sp, -1), o_m_p.reshape(bp * sp, -1), wp)
    o_m_s = _mem_decode(mq_s.reshape(bs, ss * H_M, HD_M), cache_mem_k.reshape(bs, n_mem * H_M, HD_M),
                        cache_mem_v.reshape(bs, n_mem * H_M, HD_M))
    h_s = _merge(xs, o_lat_s.reshape(bs * ss, -1), o_f_s.reshape(bs * ss, -1), o_m_s.reshape(bs * ss, -1), wp)
    y_s = _mlp(h_s, wp)

    return (y_p.reshape(bp, sp, D_MODEL), y_s.reshape(bs, ss, D_MODEL),
            lat_p.reshape(bp, sp, KV_LORA), krope_p.reshape(bp, sp, ROPE),
            fk_p.reshape(bp, sp, H_F, HD_F), fv_p.reshape(bp, sp, H_F, HD_F), logf_p.reshape(bp, sp, H_F),
            mem_k_p.reshape(bp, n_mem, H_M, HD_M), mem_v_p.reshape(bp, n_mem, H_M, HD_M),
            lat_s.reshape(bs, ss, KV_LORA), krope_s.reshape(bs, ss, ROPE),
            fk_s.reshape(bs, ss, H_F, HD_F), fv_s.reshape(bs, ss, H_F, HD_F), logf_s.reshape(bs, ss, H_F))
```
